```python
import jax, jax.numpy as jnp
from jax import lax
import numpy as np

D_MODEL = 1024
BATCH = 8
SEQ = 2048
DEPTH = 2
DEC_BATCH = 128
DEC_SEQ = 1
PAST_LEN = 16384
PAGE_SIZE = 128

D_MIX = D_MODEL
D_POOL = D_MIX // 4
D_CONV = D_MIX // 4
D_LRU = D_MIX // 2
D_IN = D_POOL + 2 * D_CONV + 2 * D_LRU
SPLITS = (D_POOL, D_POOL + D_CONV, D_POOL + 2 * D_CONV, D_POOL + 2 * D_CONV + D_LRU)

POOL_WINDOWS = (2, 4, 8, 16)
N_POOL_GROUPS = len(POOL_WINDOWS)
POOL_GROUP = D_POOL // N_POOL_GROUPS
POOL_BUF = max(POOL_WINDOWS) - 1

CONV_WIDTH = 31
CONV_BUF = CONV_WIDTH - 1

LRU_CONV_WIDTH = 4
LRU_BUF = LRU_CONV_WIDTH - 1
LRU_HEADS = 8
LRU_HEAD_DIM = D_LRU // LRU_HEADS
LRU_C = 8.0

N_EXPERTS = 32
TOP_K = 4
D_FF = D_MODEL
SWIGLU_ALPHA = 1.702
SWIGLU_LIMIT = 7.0
MOE_BLOCK = 128

RMS_EPS = 1e-6
LN_EPS = 1e-5

kernel_name = "hybrid_pool_conformer_rglru_moe_decoder_step"


def rms_norm(x, g):
    xf = x.astype(jnp.float32)
    y = xf * lax.rsqrt(jnp.mean(xf * xf, axis=-1, keepdims=True) + RMS_EPS)
    return (y * g.astype(jnp.float32)).astype(x.dtype)


def layer_norm(x, g, b):
    xf = x.astype(jnp.float32)
    mu = jnp.mean(xf, axis=-1, keepdims=True)
    var = jnp.mean(jnp.square(xf - mu), axis=-1, keepdims=True)
    y = (xf - mu) * lax.rsqrt(var + LN_EPS)
    return (y * g.astype(jnp.float32) + b.astype(jnp.float32)).astype(x.dtype)


def causal_depthwise_conv(xp, w, b):
    C = xp.shape[-1]
    y = lax.conv_general_dilated(xp, w[:, None, :].astype(xp.dtype), window_strides=(1,), padding="VALID",
                                 dimension_numbers=("NWC", "WIO", "NWC"), feature_group_count=C)
    return y + b.astype(y.dtype)


def pool_mixer(u, buf, start, w_grp, scale):
    B, S, _ = u.shape
    up = jnp.concatenate([buf.astype(u.dtype), u], axis=1)
    cs = jnp.pad(jnp.cumsum(up.astype(jnp.float32), axis=1), ((0, 0), (1, 0), (0, 0)))
    pos = start + jnp.arange(S)
    hi = cs[:, POOL_BUF + 1:POOL_BUF + 1 + S]
    means = []
    for g, w in enumerate(POOL_WINDOWS):
        ch = slice(g * POOL_GROUP, (g + 1) * POOL_GROUP)
        lo = cs[:, POOL_BUF + 1 - w:POOL_BUF + 1 - w + S, ch]
        cnt = jnp.minimum(pos + 1, w).astype(jnp.float32)[None, :, None]
        means.append((hi[..., ch] - lo) / cnt)
    d = (jnp.concatenate(means, axis=-1) - u.astype(jnp.float32)).reshape(B, S, N_POOL_GROUPS, POOL_GROUP)
    y = jnp.einsum("bsgi,gij->bsgj", d, w_grp.astype(jnp.float32)).reshape(B, S, D_POOL)
    return y * scale.astype(jnp.float32), up[:, -POOL_BUF:]


def conformer_conv(v, gate, buf, dw, db, ln_g, ln_b):
    a = v * jax.nn.sigmoid(gate)
    ap = jnp.concatenate([buf.astype(a.dtype), a], axis=1)
    y = causal_depthwise_conv(ap, dw, db)
    y = jax.nn.silu(layer_norm(y, ln_g, ln_b))
    return y, ap[:, -CONV_BUF:]


def rglru_mixer(xr, gate_br, conv_buf, h0, start, conv_w, conv_b, wa, ba, wx, bx, lam):
    B, S, C = xr.shape
    xp = jnp.concatenate([conv_buf.astype(xr.dtype), xr], axis=1)
    xc = causal_depthwise_conv(xp, conv_w, conv_b).astype(jnp.float32)
    xh = xc.reshape(B, S, LRU_HEADS, LRU_HEAD_DIM)
    r = jax.nn.sigmoid(jnp.einsum("bshi,hij->bshj", xh, wa.astype(jnp.float32)).reshape(B, S, C) + ba)
    i = jax.nn.sigmoid(jnp.einsum("bshi,hij->bshj", xh, wx.astype(jnp.float32)).reshape(B, S, C) + bx)
    log_a = -LRU_C * r * jax.nn.softplus(-lam.astype(jnp.float32))
    a = jnp.exp(log_a)
    mult = jnp.sqrt(-jnp.expm1(2.0 * log_a))
    reset = ((start + jnp.arange(S)) == 0)[None, :, None]
    a = jnp.where(reset, 0.0, a)
    mult = jnp.where(reset, 1.0, mult)
    b = mult * i * xc
    b = b.at[:, 0].add(a[:, 0] * h0.astype(jnp.float32))

    def combine(left, right):
        a1, b1 = left
        a2, b2 = right
        return a1 * a2, a2 * b1 + b2

    _, h = lax.associative_scan(combine, (a, b), axis=1)
    y = h * jax.nn.gelu(gate_br.astype(jnp.float32))
    return y, xp[:, -LRU_BUF:], h[:, -1].astype(h0.dtype)


def moe_ffn(h, router_w, router_b, w_gate, b_gate, w_up, b_up, w_down, b_down):
    B, S, D = h.shape
    xt = h.reshape(-1, D)
    T = xt.shape[0]
    TK = T * TOP_K
    logits = (xt @ router_w + router_b).astype(jnp.float32)
    top_v, top_i = lax.top_k(logits, TOP_K)
    top_w = jax.nn.softmax(top_v, axis=-1)
    flat_e = top_i.reshape(-1).astype(jnp.int32)
    flat_tok = jnp.repeat(jnp.arange(T, dtype=jnp.int32), TOP_K)
    flat_w = top_w.reshape(-1)
    order = jnp.argsort(flat_e, stable=True)
    se = flat_e[order]
    counts = jnp.bincount(flat_e, length=N_EXPERTS).astype(jnp.int32)
    pcounts = (counts + MOE_BLOCK - 1) // MOE_BLOCK * MOE_BLOCK
    starts = jnp.cumsum(counts) - counts
    pends = jnp.cumsum(pcounts)
    pstarts = pends - pcounts
    dest = pstarts[se] + jnp.arange(TK, dtype=jnp.int32) - starts[se]
    n_blocks = -(-TK // MOE_BLOCK) + N_EXPERTS
    n_rows = n_blocks * MOE_BLOCK
    row_tok = jnp.full((n_rows,), T, jnp.int32).at[dest].set(flat_tok[order])
    row_w = jnp.zeros((n_rows,), jnp.float32).at[dest].set(flat_w[order])
    block_e = jnp.minimum(jnp.searchsorted(pends, jnp.arange(n_blocks, dtype=jnp.int32) * MOE_BLOCK, side="right"),
                          N_EXPERTS - 1)
    x_pad = jnp.concatenate([xt, jnp.zeros((1, D), xt.dtype)], axis=0)
    xb = x_pad[row_tok].reshape(n_blocks, MOE_BLOCK, D)

    def expert_block(args):
        xblk, e = args
        g = xblk @ w_gate[e] + b_gate[e]
        u = xblk @ w_up[e] + b_up[e]
        g = jnp.minimum(g, SWIGLU_LIMIT)
        u = jnp.clip(u, -SWIGLU_LIMIT, SWIGLU_LIMIT)
        glu = g * jax.nn.sigmoid(SWIGLU_ALPHA * g)
        return ((u + 1.0) * glu) @ w_down[e] + b_down[e]

    yb = lax.map(expert_block, (xb, block_e)).reshape(n_rows, D)
    y = jax.ops.segment_sum(yb.astype(jnp.float32) * row_w[:, None], row_tok, num_segments=T + 1)[:T]
    return y.astype(h.dtype).reshape(B, S, D)


def trunk_layer(x, c, pool_buf, conv_buf, lru_buf, lru_h, start, p):
    mod = (jax.nn.silu(c) @ p["ada_w"] + p["ada_b"])[:, None, :]
    sh1, sc1, g1, sh2, sc2, g2 = jnp.split(mod, 6, axis=-1)
    h = rms_norm(x, p["norm1_g"]) * (1.0 + sc1) + sh1
    z = h @ p["w_in"]
    u_pool, v_conv, g_conv, x_lru, g_lru = jnp.split(z, SPLITS, axis=-1)
    o_pool, nb_pool = pool_mixer(u_pool, pool_buf, start, p["pool_w"], p["pool_scale"])
    o_conv, nb_conv = conformer_conv(v_conv, g_conv, conv_buf, p["conv_dw"], p["conv_b"],
                                     p["conv_ln_g"], p["conv_ln_b"])
    o_lru, nb_lru, nh_lru = rglru_mixer(x_lru, g_lru, lru_buf, lru_h, start, p["lru_conv_w"], p["lru_conv_b"],
                                        p["lru_wa"], p["lru_ba"], p["lru_wx"], p["lru_bx"], p["lru_lambda"])
    o = jnp.concatenate([o_pool.astype(x.dtype), o_conv.astype(x.dtype), o_lru.astype(x.dtype)], axis=-1)
    x = x + g1 * (o @ p["w_out"])
    h = rms_norm(x, p["norm2_g"]) * (1.0 + sc2) + sh2
    x = x + g2 * moe_ffn(h, p["router_w"], p["router_b"], p["moe_w_gate"], p["moe_b_gate"],
                         p["moe_w_up"], p["moe_b_up"], p["moe_w_down"], p["moe_b_down"])
    return x, nb_pool, nb_conv, nb_lru, nh_lru


def setup_inputs(seed: int = 0) -> dict:
    key = jax.random.key(seed)
    ks = iter(jax.random.split(key, 40))
    f32 = jnp.float32

    def nrm(shape, scale):
        return jax.random.normal(next(ks), shape, f32) * scale

    def gain(shape):
        return 1.0 + 0.05 * jax.random.normal(next(ks), shape, f32)

    a0 = jax.random.uniform(next(ks), (DEPTH, D_LRU), f32, 0.9, 0.999)
    a_base = a0 ** (1.0 / LRU_C)
    lam = jnp.log(a_base) - jnp.log1p(-a_base)
    return {
        "x_prompt": nrm((BATCH, SEQ, D_MODEL), 1.0),
        "x_sample": nrm((DEC_BATCH, DEC_SEQ, D_MODEL), 1.0),
        "state_pool": nrm((DEPTH, DEC_BATCH, POOL_BUF, D_POOL), 1.0),
        "state_conv": nrm((DEPTH, DEC_BATCH, CONV_BUF, D_CONV), 0.5),
        "state_lru_conv": nrm((DEPTH, DEC_BATCH, LRU_BUF, D_LRU), 1.0),
        "state_lru_h": nrm((DEPTH, DEC_BATCH, D_LRU), 0.5),
        "c_prompt": nrm((BATCH, D_MODEL), 1.0),
        "c_sample": nrm((DEC_BATCH, D_MODEL), 1.0),
        "ada_w": nrm((DEPTH, D_MODEL, 6 * D_MODEL), 0.5 * D_MODEL ** -0.5),
        "ada_b": nrm((DEPTH, 6 * D_MODEL), 0.02),
        "norm1_g": gain((DEPTH, D_MODEL)),
        "w_in": nrm((DEPTH, D_MODEL, D_IN), D_MODEL ** -0.5),
        "pool_w": nrm((DEPTH, N_POOL_GROUPS, POOL_GROUP, POOL_GROUP), POOL_GROUP ** -0.5),
        "pool_scale": gain((DEPTH, D_POOL)),
        "conv_dw": nrm((DEPTH, CONV_WIDTH, D_CONV), CONV_WIDTH ** -0.5),
        "conv_b": nrm((DEPTH, D_CONV), 0.02),
        "conv_ln_g": gain((DEPTH, D_CONV)),
        "conv_ln_b": nrm((DEPTH, D_CONV), 0.02),
        "lru_conv_w": nrm((DEPTH, LRU_CONV_WIDTH, D_LRU), LRU_CONV_WIDTH ** -0.5),
        "lru_conv_b": nrm((DEPTH, D_LRU), 0.02),
        "lru_wa": nrm((DEPTH, LRU_HEADS, LRU_HEAD_DIM, LRU_HEAD_DIM), LRU_HEAD_DIM ** -0.5),
        "lru_ba": nrm((DEPTH, D_LRU), 0.02),
        "lru_wx": nrm((DEPTH, LRU_HEADS, LRU_HEAD_DIM, LRU_HEAD_DIM), LRU_HEAD_DIM ** -0.5),
        "lru_bx": nrm((DEPTH, D_LRU), 0.02),
        "lru_lambda": lam,
        "w_out": nrm((DEPTH, D_MIX, D_MODEL), D_MIX ** -0.5),
        "norm2_g": gain((DEPTH, D_MODEL)),
        "router_w": nrm((DEPTH, D_MODEL, N_EXPERTS), D_MODEL ** -0.5),
        "router_b": nrm((DEPTH, N_EXPERTS), 0.01),
        "moe_w_gate": nrm((DEPTH, N_EXPERTS, D_MODEL, D_FF), D_MODEL ** -0.5),
        "moe_b_gate": nrm((DEPTH, N_EXPERTS, D_FF), 0.02),
        "moe_w_up": nrm((DEPTH, N_EXPERTS, D_MODEL, D_FF), D_MODEL ** -0.5),
        "moe_b_up": nrm((DEPTH, N_EXPERTS, D_FF), 0.02),
        "moe_w_down": nrm((DEPTH, N_EXPERTS, D_FF, D_MODEL), D_FF ** -0.5),
        "moe_b_down": nrm((DEPTH, N_EXPERTS, D_MODEL), 0.02),
        "final_g": gain((D_MODEL,)),
    }


def reference(x_prompt, x_sample, state_pool, state_conv, state_lru_conv, state_lru_h, c_prompt, c_sample,
              ada_w, ada_b, norm1_g, w_in, pool_w, pool_scale, conv_dw, conv_b, conv_ln_g, conv_ln_b,
              lru_conv_w, lru_conv_b, lru_wa, lru_ba, lru_wx, lru_bx, lru_lambda, w_out, norm2_g,
              router_w, router_b, moe_w_gate, moe_b_gate, moe_w_up, moe_b_up, moe_w_down, moe_b_down, final_g):
    layers = [dict(ada_w=ada_w[l], ada_b=ada_b[l], norm1_g=norm1_g[l], w_in=w_in[l], pool_w=pool_w[l],
                   pool_scale=pool_scale[l], conv_dw=conv_dw[l], conv_b=conv_b[l], conv_ln_g=conv_ln_g[l],
                   conv_ln_b=conv_ln_b[l], lru_conv_w=lru_conv_w[l], lru_conv_b=lru_conv_b[l], lru_wa=lru_wa[l],
                   lru_ba=lru_ba[l], lru_wx=lru_wx[l], lru_bx=lru_bx[l], lru_lambda=lru_lambda[l], w_out=w_out[l],
                   norm2_g=norm2_g[l], router_w=router_w[l], router_b=router_b[l], moe_w_gate=moe_w_gate[l],
                   moe_b_gate=moe_b_gate[l], moe_w_up=moe_w_up[l], moe_b_up=moe_b_up[l],
                   moe_w_down=moe_w_down[l], moe_b_down=moe_b_down[l])
              for l in range(DEPTH)]

    def run(x, c, pool_s, conv_s, lru_conv_s, lru_h_s, start):
        np_, nc_, nl_, nh_ = [], [], [], []
        for l in range(DEPTH):
            x, nb_pool, nb_conv, nb_lru, nh_lru = trunk_layer(x, c, pool_s[l], conv_s[l], lru_conv_s[l],
                                                               lru_h_s[l], start, layers[l])
            np_.append(nb_pool)
            nc_.append(nb_conv)
            nl_.append(nb_lru)
            nh_.append(nh_lru)
        return rms_norm(x, final_g), jnp.stack(np_), jnp.stack(nc_), jnp.stack(nl_), jnp.stack(nh_)

    B = x_prompt.shape[0]
    dt = x_prompt.dtype
    y_prompt, pool_p, conv_p, lruc_p, lruh_p = run(
        x_prompt, c_prompt,
        jnp.zeros((DEPTH, B, POOL_BUF, D_POOL), dt), jnp.zeros((DEPTH, B, CONV_BUF, D_CONV), dt),
        jnp.zeros((DEPTH, B, LRU_BUF, D_LRU), dt), jnp.zeros((DEPTH, B, D_LRU), dt), 0)
    y_sample, pool_s, conv_s, lruc_s, lruh_s = run(
        x_sample, c_sample, state_pool, state_conv, state_lru_conv, state_lru_h, PAST_LEN)
    return (y_prompt, y_sample, pool_p, conv_p, lruc_p, lruh_p, pool_s, conv_s, lruc_s, lruh_s)
```

```python
import functools

import jax
import jax.numpy as jnp
from jax import lax
from jax.experimental import pallas as pl
from jax.experimental.pallas import tpu as pltpu

F32 = jnp.float32
BF16 = jnp.bfloat16
I32 = jnp.int32

POOL_WINDOWS = (2, 4, 8, 16)
POOL_BUF = max(POOL_WINDOWS) - 1
CONV_WIDTH = 31
CONV_BUF = CONV_WIDTH - 1
LRU_CONV_WIDTH = 4
LRU_BUF = LRU_CONV_WIDTH - 1
LRU_C = 8.0
TOP_K = 4
SWIGLU_ALPHA = 1.702
SWIGLU_LIMIT = 7.0
RMS_EPS = 1e-6
LN_EPS = 1e-5
PAST_LEN = 16384

V7X_SUBLANES = 8
V7X_LANES = 128
V7X_MXU_DIM = 256
V7X_VMEM_LIMIT_BYTES = 56 * 1024 * 1024

SEQ_TILE = 256
MOE_BLOCK = 256
COMBINE_TILE = 256
HIST_POOL = 16
HIST_CONV = 32
HIST_LRU = 8


def _rms_mod(x, g, scale, shift):
    ms = jnp.mean(x * x, axis=-1, keepdims=True)
    return x * lax.rsqrt(ms + RMS_EPS) * g * (1.0 + scale) + shift


def _pool_window_select(accs, lane, pool_group):
    n = len(POOL_WINDOWS)
    wsum = accs[POOL_WINDOWS[-1]]
    wlen = jnp.full(lane.shape, POOL_WINDOWS[-1], I32)
    for g in reversed(range(n - 1)):
        m = lane < (g + 1) * pool_group
        wsum = jnp.where(m, accs[POOL_WINDOWS[g]], wsum)
        wlen = jnp.where(m, POOL_WINDOWS[g], wlen)
    return wsum, wlen


def _layer_norm_silu(y, g, b):
    mu = jnp.mean(y, axis=-1, keepdims=True)
    yc = y - mu
    var = jnp.mean(yc * yc, axis=-1, keepdims=True)
    return jax.nn.silu(yc * lax.rsqrt(var + LN_EPS) * g + b)


def _softplus(x):
    return jnp.maximum(x, 0.0) + jnp.log1p(jnp.exp(-jnp.abs(x)))


def _lru_gates(xc, wa_ref, ba, wx_ref, bx, lam, reset):
    half = xc.shape[1] // 2
    xb = xc.astype(BF16)

    def heads(w_ref):
        lo = jnp.dot(xb[:, :half], w_ref[0], preferred_element_type=F32)
        hi = jnp.dot(xb[:, half:], w_ref[1], preferred_element_type=F32)
        return jnp.concatenate([lo, hi], axis=1)

    r = jax.nn.sigmoid(heads(wa_ref) + ba)
    i = jax.nn.sigmoid(heads(wx_ref) + bx)
    log_a = -LRU_C * r * _softplus(-lam)
    a = jnp.exp(log_a)
    mult = jnp.sqrt(1.0 - a * a)
    if reset is not None:
        a = jnp.where(reset, 0.0, a)
        mult = jnp.where(reset, 1.0, mult)
    return a, mult * i * xc


def _scan_rows(a, b, h0):
    rows, c = a.shape
    groups = rows // V7X_SUBLANES
    a3 = a.reshape(groups, V7X_SUBLANES, c)
    b3 = b.reshape(groups, V7X_SUBLANES, c)
    sub = lax.broadcasted_iota(I32, a3.shape, 1)
    d = 1
    while d < V7X_SUBLANES:
        ar = pltpu.roll(a3, d, axis=1)
        br = pltpu.roll(b3, d, axis=1)
        m = sub >= d
        b3 = jnp.where(m, a3 * br + b3, b3)
        a3 = jnp.where(m, a3 * ar, a3)
        d *= 2
    hp = h0
    outs = []
    for g in range(groups):
        hg = a3[g] * hp + b3[g]
        outs.append(hg)
        hp = hg[V7X_SUBLANES - 1:V7X_SUBLANES, :]
    return jnp.concatenate(outs, axis=0)


def _route(h2, rwt_ref, rb_ref, base):
    n = h2.shape[0]
    n_exp = rwt_ref.shape[0]
    logits = lax.dot_general(rwt_ref[...], h2, (((1,), (1,)), ((), ())),
                             precision=lax.Precision.HIGHEST, preferred_element_type=F32) + rb_ref[...]
    eio = lax.broadcasted_iota(I32, (n_exp, n), 0).astype(F32)
    l = logits
    vals, idxs, sels = [], [], []
    for _ in range(TOP_K):
        m = jnp.max(l, axis=0, keepdims=True)
        idx = jnp.min(jnp.where(l == m, eio, float(n_exp)), axis=0, keepdims=True)
        sel = eio == idx
        vals.append(m)
        idxs.append(idx)
        sels.append(sel)
        l = jnp.where(sel, -jnp.inf, l)
    exps = [jnp.exp(v - vals[0]) for v in vals]
    den = exps[0]
    for e in exps[1:]:
        den = den + e
    ws = [e / den for e in exps]
    onehot = jnp.zeros((n_exp, n), F32)
    for sel in sels:
        onehot = jnp.where(sel, 1.0, onehot)
    before = lax.broadcasted_iota(I32, (n, n), 0) < lax.broadcasted_iota(I32, (n, n), 1)
    tri = jnp.where(before, 1.0, 0.0).astype(BF16)
    prior = jnp.dot(onehot.astype(BF16), tri, preferred_element_type=F32)
    tot = prior + jnp.concatenate([base] * (n // V7X_LANES), axis=1)
    ranks = [jnp.sum(jnp.where(sel, tot, 0.0), axis=0, keepdims=True) for sel in sels]
    new_base = base + jnp.broadcast_to(jnp.sum(onehot, axis=1, keepdims=True), base.shape)
    ri = jnp.concatenate(idxs + ranks, axis=0).astype(I32)
    rw = jnp.concatenate(ws + [jnp.zeros((TOP_K, n), F32)], axis=0)
    return ri, rw, new_base


def _mix_prompt_kernel(x_ref, mod_ref, n1g_ref, win_ref, poolw_ref, pools_ref, cdw_ref, cb_ref, lng_ref, lnb_ref,
                       lcw_ref, lcb_ref, wa_ref, ba_ref, wx_ref, bx_ref, lam_ref, wout_ref, n2g_ref, rwt_ref,
                       rb_ref,
                       x1_ref, h2_ref, ri_ref, rw_ref, cnt_ref, npool_ref, nconv_ref, nlc_ref, nlh_ref,
                       e_s, a_s, xe_s, h_s, c_s, *, ts, d_pool, d_conv, d_lru):
    b = pl.program_id(0)
    s = pl.program_id(1)

    @pl.when(s == 0)
    def _():
        e_s[0:HIST_POOL, :] = jnp.zeros((HIST_POOL, d_pool), F32)
        a_s[0:HIST_CONV, :] = jnp.zeros((HIST_CONV, d_conv), F32)
        xe_s[0:HIST_LRU, :] = jnp.zeros((HIST_LRU, d_lru), F32)
        h_s[...] = jnp.zeros(h_s.shape, F32)

    @pl.when(jnp.logical_and(b == 0, s == 0))
    def _():
        c_s[...] = jnp.zeros(c_s.shape, F32)

    x = x_ref[...]
    m = mod_ref[0]
    sh1, sc1, g1, sh2, sc2, g2 = [m[i:i + 1] for i in range(6)]
    h = _rms_mod(x, n1g_ref[...], sc1, sh1)
    z = jnp.dot(h.astype(BF16), win_ref[...], preferred_element_type=F32)
    o0 = 0
    u = z[:, o0:o0 + d_pool]
    o0 += d_pool
    v = z[:, o0:o0 + d_conv]
    o0 += d_conv
    gc = z[:, o0:o0 + d_conv]
    o0 += d_conv
    xr = z[:, o0:o0 + d_lru]
    o0 += d_lru
    gl = z[:, o0:o0 + d_lru]

    e_s[HIST_POOL:HIST_POOL + ts, :] = u
    acc = u
    accs = {}
    for j in range(1, POOL_BUF + 1):
        acc = acc + e_s[pl.ds(HIST_POOL - j, ts), :]
        if j + 1 in POOL_WINDOWS:
            accs[j + 1] = acc
    lane = lax.broadcasted_iota(I32, (ts, d_pool), 1)
    pos = s * ts + lax.broadcasted_iota(I32, (ts, d_pool), 0)
    wsum, wlen = _pool_window_select(accs, lane, d_pool // len(POOL_WINDOWS))
    cnt = jnp.minimum(pos + 1, wlen).astype(F32)
    dmean = wsum / cnt - u
    o_pool = jnp.dot(dmean.astype(BF16), poolw_ref[...], preferred_element_type=F32) * pools_ref[...]
    npool_ref[0] = e_s[pl.ds(HIST_POOL + ts - POOL_BUF, POOL_BUF), :]
    e_s[0:HIST_POOL, :] = e_s[pl.ds(ts, HIST_POOL), :]

    a = v * jax.nn.sigmoid(gc)
    a_s[HIST_CONV:HIST_CONV + ts, :] = a
    y = jnp.broadcast_to(cb_ref[...], (ts, d_conv))
    for j in range(CONV_WIDTH):
        y = y + cdw_ref[j:j + 1, :] * a_s[pl.ds(HIST_CONV - CONV_BUF + j, ts), :]
    o_conv = _layer_norm_silu(y, lng_ref[...], lnb_ref[...])
    nconv_ref[0] = a_s[pl.ds(HIST_CONV + ts - CONV_BUF, CONV_BUF), :]
    a_s[0:HIST_CONV, :] = a_s[pl.ds(ts, HIST_CONV), :]

    xe_s[HIST_LRU:HIST_LRU + ts, :] = xr
    xc = jnp.broadcast_to(lcb_ref[...], (ts, d_lru))
    for j in range(LRU_CONV_WIDTH):
        xc = xc + lcw_ref[j:j + 1, :] * xe_s[pl.ds(HIST_LRU - LRU_BUF + j, ts), :]
    reset = (s * ts + lax.broadcasted_iota(I32, (ts, d_lru), 0)) == 0
    a_t, b_t = _lru_gates(xc, wa_ref, ba_ref[...], wx_ref, bx_ref[...], lam_ref[...], reset)
    hseq = _scan_rows(a_t, b_t, h_s[0:1, :])
    o_lru = hseq * jax.nn.gelu(gl)
    nlc_ref[0] = xe_s[pl.ds(HIST_LRU + ts - LRU_BUF, LRU_BUF), :]
    nlh_ref[0] = hseq[ts - 1:ts, :]
    h_s[...] = jnp.broadcast_to(hseq[ts - 1:ts, :], h_s.shape)
    xe_s[0:HIST_LRU, :] = xe_s[pl.ds(ts, HIST_LRU), :]

    o = jnp.concatenate([o_pool, o_conv, o_lru], axis=1).astype(BF16)
    x1 = x + g1 * jnp.dot(o, wout_ref[...], preferred_element_type=F32)
    x1_ref[...] = x1
    h2 = _rms_mod(x1, n2g_ref[...], sc2, sh2)
    h2_ref[...] = h2

    ri, rw, new_base = _route(h2, rwt_ref, rb_ref, c_s[...])
    ri_ref[...] = ri
    rw_ref[...] = rw
    c_s[...] = new_base
    cnt_ref[...] = new_base


def _mix_sample_kernel(x_ref, mod_ref, n1g_ref, win_ref, poolw_ref, pools_ref, cdw_ref, cb_ref, lng_ref, lnb_ref,
                       lcw_ref, lcb_ref, wa_ref, ba_ref, wx_ref, bx_ref, lam_ref, wout_ref, n2g_ref, rwt_ref,
                       rb_ref, sp_ref, sc_ref, sl_ref, sh_ref, base_ref,
                       x1_ref, h2_ref, ri_ref, rw_ref, cnt_ref, npool_ref, nconv_ref, nlc_ref, nlh_ref,
                       *, start, d_model, d_pool, d_conv, d_lru):
    n = x_ref.shape[0]
    x = x_ref[...]
    m = mod_ref[...]
    sh1, sc1, g1, sh2, sc2, g2 = [m[:, i * d_model:(i + 1) * d_model] for i in range(6)]
    h = _rms_mod(x, n1g_ref[...], sc1, sh1)
    z = jnp.dot(h.astype(BF16), win_ref[...], preferred_element_type=F32)
    o0 = 0
    u = z[:, o0:o0 + d_pool]
    o0 += d_pool
    v = z[:, o0:o0 + d_conv]
    o0 += d_conv
    gc = z[:, o0:o0 + d_conv]
    o0 += d_conv
    xr = z[:, o0:o0 + d_lru]
    o0 += d_lru
    gl = z[:, o0:o0 + d_lru]

    acc = u
    accs = {}
    for j in range(1, POOL_BUF + 1):
        acc = acc + sp_ref[POOL_BUF - j]
        if j + 1 in POOL_WINDOWS:
            accs[j + 1] = acc
    lane = lax.broadcasted_iota(I32, (n, d_pool), 1)
    wsum, wlen = _pool_window_select(accs, lane, d_pool // len(POOL_WINDOWS))
    cnt = jnp.minimum(start + 1, wlen).astype(F32)
    dmean = wsum / cnt - u
    o_pool = jnp.dot(dmean.astype(BF16), poolw_ref[...], preferred_element_type=F32) * pools_ref[...]
    for j in range(POOL_BUF - 1):
        npool_ref[j] = sp_ref[j + 1]
    npool_ref[POOL_BUF - 1] = u

    a = v * jax.nn.sigmoid(gc)
    y = cb_ref[...] + cdw_ref[CONV_BUF:CONV_WIDTH, :] * a
    for j in range(CONV_BUF):
        y = y + cdw_ref[j:j + 1, :] * sc_ref[j]
    o_conv = _layer_norm_silu(y, lng_ref[...], lnb_ref[...])
    for j in range(CONV_BUF - 1):
        nconv_ref[j] = sc_ref[j + 1]
    nconv_ref[CONV_BUF - 1] = a

    xc = lcb_ref[...] + lcw_ref[LRU_BUF:LRU_CONV_WIDTH, :] * xr
    for j in range(LRU_BUF):
        xc = xc + lcw_ref[j:j + 1, :] * sl_ref[j]
    reset = jnp.full((n, d_lru), True) if start == 0 else None
    a_t, b_t = _lru_gates(xc, wa_ref, ba_ref[...], wx_ref, bx_ref[...], lam_ref[...], reset)
    hnew = a_t * sh_ref[...] + b_t
    o_lru = hnew * jax.nn.gelu(gl)
    for j in range(LRU_BUF - 1):
        nlc_ref[j] = sl_ref[j + 1]
    nlc_ref[LRU_BUF - 1] = xr
    nlh_ref[...] = hnew

    o = jnp.concatenate([o_pool, o_conv, o_lru], axis=1).astype(BF16)
    x1 = x + g1 * jnp.dot(o, wout_ref[...], preferred_element_type=F32)
    x1_ref[...] = x1
    h2 = _rms_mod(x1, n2g_ref[...], sc2, sh2)
    h2_ref[...] = h2
    ri, rw, new_base = _route(h2, rwt_ref, rb_ref, base_ref[...])
    ri_ref[...] = ri
    rw_ref[...] = rw
    cnt_ref[...] = new_base


def _adaln_kernel(c_ref, w_ref, b_ref, o_ref):
    c = c_ref[...]
    o_ref[0] = jnp.dot(jax.nn.silu(c).astype(BF16), w_ref[0].astype(BF16), preferred_element_type=F32) + b_ref[0]


def _adaln(c_all, ada_w, ada_b):
    depth, d, n6 = ada_w.shape
    nb = c_all.shape[0]
    tn = d
    return pl.pallas_call(
        _adaln_kernel,
        grid=(depth, n6 // tn),
        in_specs=[pl.BlockSpec((nb, d), lambda l, j: (0, 0)),
                  pl.BlockSpec((1, d, tn), lambda l, j: (l, 0, j)),
                  pl.BlockSpec((1, 1, tn), lambda l, j: (l, 0, j))],
        out_specs=pl.BlockSpec((1, nb, tn), lambda l, j: (l, 0, j)),
        out_shape=jax.ShapeDtypeStruct((depth, nb, n6), F32),
        compiler_params=pltpu.CompilerParams(dimension_semantics=("arbitrary", "arbitrary")),
        name="adaln",
    )(c_all, ada_w, ada_b.reshape(depth, 1, n6))


def _const_spec(shape):
    nd = len(shape)
    return pl.BlockSpec(shape, lambda *_: (0,) * nd)


def _layer_weights(p, l):
    d_pool = p["pool_scale"].shape[1]
    d_lru = p["lru_lambda"].shape[1]
    ng, pg, _ = p["pool_w"].shape[1:]
    pool_bd = jnp.zeros((d_pool, d_pool), F32)
    for g in range(ng):
        pool_bd = pool_bd.at[g * pg:(g + 1) * pg, g * pg:(g + 1) * pg].set(p["pool_w"][l, g])

    def head_tiles(w):
        nh, hd, _ = w.shape
        half = d_lru // 2
        per = half // hd
        tiles = []
        for t in range(2):
            blk = jnp.zeros((half, half), F32)
            for q in range(per):
                blk = blk.at[q * hd:(q + 1) * hd, q * hd:(q + 1) * hd].set(w[t * per + q])
            tiles.append(blk)
        return jnp.stack(tiles).astype(BF16)

    row = lambda a: a[l].reshape(1, -1)
    return [row(p["norm1_g"]), p["w_in"][l].astype(BF16), pool_bd.astype(BF16), row(p["pool_scale"]),
            p["conv_dw"][l], row(p["conv_b"]), row(p["conv_ln_g"]), row(p["conv_ln_b"]),
            p["lru_conv_w"][l], row(p["lru_conv_b"]), head_tiles(p["lru_wa"][l]), row(p["lru_ba"]),
            head_tiles(p["lru_wx"][l]), row(p["lru_bx"]), row(p["lru_lambda"]),
            p["w_out"][l].astype(BF16), row(p["norm2_g"]), p["router_w"][l].T, p["router_b"][l].reshape(-1, 1)]


def _mix_prompt(x, mod, weights, n_batch, seq):
    t, d_model = x.shape
    ts = SEQ_TILE
    n_s = seq // ts
    d_pool = weights[3].shape[1]
    d_conv = weights[5].shape[1]
    d_lru = weights[9].shape[1]
    n_exp = weights[17].shape[0]
    tok = lambda b, s: (b * n_s + s, 0)
    lanes = lambda b, s: (0, b * n_s + s)
    per_b = lambda b, s: (b, 0, 0)
    in_specs = [pl.BlockSpec((ts, d_model), tok), pl.BlockSpec((1, 6, d_model), per_b)]
    in_specs += [_const_spec(w.shape) for w in weights]
    out_shape = (jax.ShapeDtypeStruct((t, d_model), F32), jax.ShapeDtypeStruct((t, d_model), F32),
                 jax.ShapeDtypeStruct((2 * TOP_K, t), I32), jax.ShapeDtypeStruct((2 * TOP_K, t), F32),
                 jax.ShapeDtypeStruct((n_exp, V7X_LANES), F32),
                 jax.ShapeDtypeStruct((n_batch, POOL_BUF, d_pool), F32),
                 jax.ShapeDtypeStruct((n_batch, CONV_BUF, d_conv), F32),
                 jax.ShapeDtypeStruct((n_batch, LRU_BUF, d_lru), F32),
                 jax.ShapeDtypeStruct((n_batch, 1, d_lru), F32))
    out_specs = (pl.BlockSpec((ts, d_model), tok), pl.BlockSpec((ts, d_model), tok),
                 pl.BlockSpec((2 * TOP_K, ts), lanes), pl.BlockSpec((2 * TOP_K, ts), lanes),
                 _const_spec((n_exp, V7X_LANES)),
                 pl.BlockSpec((1, POOL_BUF, d_pool), per_b), pl.BlockSpec((1, CONV_BUF, d_conv), per_b),
                 pl.BlockSpec((1, LRU_BUF, d_lru), per_b), pl.BlockSpec((1, 1, d_lru), per_b))
    scratch = [pltpu.VMEM((HIST_POOL + ts, d_pool), F32), pltpu.VMEM((HIST_CONV + ts, d_conv), F32),
               pltpu.VMEM((HIST_LRU + ts, d_lru), F32), pltpu.VMEM((V7X_SUBLANES, d_lru), F32),
               pltpu.VMEM((n_exp, V7X_LANES), F32)]
    return pl.pallas_call(
        functools.partial(_mix_prompt_kernel, ts=ts, d_pool=d_pool, d_conv=d_conv, d_lru=d_lru),
        grid=(n_batch, n_s), in_specs=in_specs, out_specs=out_specs, out_shape=out_shape, scratch_shapes=scratch,
        compiler_params=pltpu.CompilerParams(dimension_semantics=("arbitrary", "arbitrary"),
                                             vmem_limit_bytes=V7X_VMEM_LIMIT_BYTES),
        name="mix_prompt",
    )(x, mod, *weights)


def _mix_sample(x, mod, weights, st_pool, st_conv, st_lruc, st_lruh, base, start):
    n, d_model = x.shape
    d_pool = weights[3].shape[1]
    d_conv = weights[5].shape[1]
    d_lru = weights[9].shape[1]
    n_exp = weights[17].shape[0]
    ins = [x, mod] + list(weights) + [st_pool, st_conv, st_lruc, st_lruh, base]
    in_specs = [_const_spec(a.shape) for a in ins]
    out_shape = (jax.ShapeDtypeStruct((n, d_model), F32), jax.ShapeDtypeStruct((n, d_model), F32),
                 jax.ShapeDtypeStruct((2 * TOP_K, n), I32), jax.ShapeDtypeStruct((2 * TOP_K, n), F32),
                 jax.ShapeDtypeStruct((n_exp, V7X_LANES), F32),
                 jax.ShapeDtypeStruct(st_pool.shape, F32), jax.ShapeDtypeStruct(st_conv.shape, F32),
                 jax.ShapeDtypeStruct(st_lruc.shape, F32), jax.ShapeDtypeStruct(st_lruh.shape, F32))
    out_specs = (_const_spec((n, d_model)), _const_spec((n, d_model)),
                 _const_spec((2 * TOP_K, n)), _const_spec((2 * TOP_K, n)), _const_spec((n_exp, V7X_LANES)),
                 _const_spec(st_pool.shape), _const_spec(st_conv.shape), _const_spec(st_lruc.shape),
                 _const_spec(st_lruh.shape))
    return pl.pallas_call(
        functools.partial(_mix_sample_kernel, start=start, d_model=d_model, d_pool=d_pool, d_conv=d_conv,
                          d_lru=d_lru),
        grid=(1,), in_specs=in_specs, out_specs=out_specs, out_shape=out_shape,
        compiler_params=pltpu.CompilerParams(dimension_semantics=("arbitrary",),
                                             vmem_limit_bytes=V7X_VMEM_LIMIT_BYTES),
        name="mix_sample",
    )(*ins)


def _expert_kernel(be_ref, np_ref, nv_ref, na_ref, slot_ref, slot_nx_ref, h2p_ref, h2s_ref, wg_ref, wu_ref, wd_ref,
                   bg_ref, bu_ref, bd_ref, ys_ref, xbuf, ybuf, wbf, gsem, ssem, *, bm):
    i = pl.program_id(0)
    n_active = na_ref[0]
    cur = i % 2
    t_p = h2p_ref.shape[0]

    def start_gather(sref, blk, buf_slot):
        def row_copy(src_ref, tok, j):
            pltpu.make_async_copy(src_ref.at[pl.ds(tok, 1)], xbuf.at[buf_slot, pl.ds(j, 1)],
                                  gsem.at[buf_slot]).start()

        def body_p(j, c):
            row_copy(h2p_ref, sref[0, 0, j] // TOP_K, j)
            return c

        def body_s(j, c):
            row_copy(h2s_ref, sref[0, 0, j] // TOP_K - t_p, j)
            return c

        lax.fori_loop(0, np_ref[blk], body_p, 0)
        lax.fori_loop(np_ref[blk], nv_ref[blk], body_s, 0)

    def wait_rows(buf, sem, n_rows, buf_slot):
        n_tiled = pl.multiple_of(n_rows // V7X_SUBLANES * V7X_SUBLANES, V7X_SUBLANES)

        @pl.when(n_tiled > 0)
        def _():
            rows = buf.at[buf_slot, pl.ds(0, n_tiled)]
            pltpu.make_async_copy(rows, rows, sem.at[buf_slot]).wait()

        def body(j, c):
            row = buf.at[buf_slot, pl.ds(0, 1)]
            pltpu.make_async_copy(row, row, sem.at[buf_slot]).wait()
            return c
        lax.fori_loop(0, n_rows - n_tiled, body, 0)

    @pl.when(i == 0)
    def _():
        xbuf[...] = jnp.zeros(xbuf.shape, F32)
        start_gather(slot_ref, 0, 0)

    @pl.when(i + 1 < n_active)
    def _():
        start_gather(slot_nx_ref, i + 1, 1 - cur)

    @pl.when(i < n_active)
    def _():
        changed = jnp.logical_or(i == 0, be_ref[i] != be_ref[jnp.maximum(i - 1, 0)])

        @pl.when(changed)
        def _():
            wbf[0] = wg_ref[0].astype(BF16)
            wbf[1] = wu_ref[0].astype(BF16)
            wbf[2] = wd_ref[0].astype(BF16)

        wait_rows(xbuf, gsem, nv_ref[i], cur)

        @pl.when(i >= 2)
        def _():
            wait_rows(ybuf, ssem, nv_ref[jnp.maximum(i - 2, 0)], cur)

        x = xbuf[cur].astype(BF16)
        g = jnp.dot(x, wbf[0], preferred_element_type=F32) + bg_ref[0]
        u = jnp.dot(x, wbf[1], preferred_element_type=F32) + bu_ref[0]
        g = jnp.minimum(g, SWIGLU_LIMIT)
        u = jnp.clip(u, -SWIGLU_LIMIT, SWIGLU_LIMIT)
        act = (u + 1.0) * (g * jax.nn.sigmoid(SWIGLU_ALPHA * g))
        ybuf[cur] = jnp.dot(act.astype(BF16), wbf[2], preferred_element_type=F32) + bd_ref[0]

        def body(j, c):
            sl = slot_ref[0, 0, j]
            pltpu.make_async_copy(ybuf.at[cur, pl.ds(j, 1)], ys_ref.at[sl % TOP_K, pl.ds(sl // TOP_K, 1)],
                                  ssem.at[cur]).start()
            return c
        lax.fori_loop(0, nv_ref[i], body, 0)

        @pl.when(i == n_active - 1)
        def _():
            wait_rows(ybuf, ssem, nv_ref[i], cur)

            @pl.when(i >= 1)
            def _():
                wait_rows(ybuf, ssem, nv_ref[jnp.maximum(i - 1, 0)], 1 - cur)


def _experts(h2_p, h2_s, slots, block_e, n_prompt, n_valid, n_active, wg, wu, wd, bg, bu, bd):
    t_all = h2_p.shape[0] + h2_s.shape[0]
    d = h2_p.shape[1]
    n_blocks, _, bm = slots.shape
    n_exp, _, d_ff = wg.shape
    cur_blk = lambda i, *_: (i, 0, 0)
    nxt_blk = lambda i, *_: (jnp.minimum(i + 1, n_blocks - 1), 0, 0)
    w_blk = lambda i, be, *_: (be[i], 0, 0)
    smem_spec = lambda im: pl.BlockSpec((1, 1, bm), im, memory_space=pltpu.SMEM)
    grid_spec = pltpu.PrefetchScalarGridSpec(
        num_scalar_prefetch=4, grid=(n_blocks,),
        in_specs=[smem_spec(cur_blk), smem_spec(nxt_blk), pl.BlockSpec(memory_space=pl.ANY),
                  pl.BlockSpec(memory_space=pl.ANY),
                  pl.BlockSpec((1, d, d_ff), w_blk), pl.BlockSpec((1, d, d_ff), w_blk),
                  pl.BlockSpec((1, d_ff, d), w_blk),
                  pl.BlockSpec((1, 1, d_ff), w_blk), pl.BlockSpec((1, 1, d_ff), w_blk),
                  pl.BlockSpec((1, 1, d), w_blk)],
        out_specs=pl.BlockSpec(memory_space=pl.ANY),
        scratch_shapes=[pltpu.VMEM((2, bm, d), F32), pltpu.VMEM((2, bm, d), F32), pltpu.VMEM((3, d, d_ff), BF16),
                        pltpu.SemaphoreType.DMA((2,)), pltpu.SemaphoreType.DMA((2,))])
    return pl.pallas_call(
        functools.partial(_expert_kernel, bm=bm),
        grid_spec=grid_spec,
        out_shape=jax.ShapeDtypeStruct((TOP_K, t_all, d), F32),
        compiler_params=pltpu.CompilerParams(dimension_semantics=("arbitrary",),
                                             vmem_limit_bytes=V7X_VMEM_LIMIT_BYTES),
        name="experts",
    )(block_e, n_prompt, n_valid, n_active, slots, slots, h2_p, h2_s, wg, wu, wd, bg.reshape(n_exp, 1, d_ff),
      bu.reshape(n_exp, 1, d_ff), bd.reshape(n_exp, 1, d))


def _combine_kernel(x1_ref, ys_ref, w_ref, g2_ref, fg_ref, o_ref, *, final):
    y = ys_ref[0] * w_ref[:, 0:1]
    for k in range(1, TOP_K):
        y = y + ys_ref[k] * w_ref[:, k:k + 1]
    x2 = x1_ref[...] + g2_ref[...] * y
    if final:
        ms = jnp.mean(x2 * x2, axis=-1, keepdims=True)
        x2 = x2 * lax.rsqrt(ms + RMS_EPS) * fg_ref[...]
    o_ref[...] = x2


def _combine(x1, ys, w_tok, g2, final_g, row0, rows_per_gate, final):
    t, d = x1.shape
    tt = min(COMBINE_TILE, t) if rows_per_gate == 1 else min(COMBINE_TILE, t, rows_per_gate)
    assert row0 % tt == 0 and t % tt == 0
    blk0 = row0 // tt
    if rows_per_gate == 1:
        g2_spec = pl.BlockSpec((tt, d), lambda i: (i, 0))
    else:
        g2 = g2.reshape(-1, 1, d)
        g2_spec = pl.BlockSpec((None, 1, d), lambda i: ((i * tt) // rows_per_gate, 0, 0))
    return pl.pallas_call(
        functools.partial(_combine_kernel, final=final),
        grid=(t // tt,),
        in_specs=[pl.BlockSpec((tt, d), lambda i: (i, 0)),
                  pl.BlockSpec((TOP_K, tt, d), lambda i: (0, blk0 + i, 0)),
                  pl.BlockSpec((tt, TOP_K), lambda i: (i, 0)),
                  g2_spec, _const_spec((1, d))],
        out_specs=pl.BlockSpec((tt, d), lambda i: (i, 0)),
        out_shape=jax.ShapeDtypeStruct((t, d), F32),
        compiler_params=pltpu.CompilerParams(dimension_semantics=("arbitrary",)),
        name="combine",
    )(x1, ys, w_tok, g2, final_g.reshape(1, d))


def _routing_plan(topi, rank, counts_prompt, counts, bm, n_blocks):
    t_all = topi.shape[1]
    n_exp = counts.shape[0]
    pcounts = (counts + bm - 1) // bm * bm
    pends = jnp.cumsum(pcounts)
    pstarts = pends - pcounts
    dest = pstarts[topi] + rank
    slot_ids = jnp.arange(t_all, dtype=I32)[None, :] * TOP_K + jnp.arange(TOP_K, dtype=I32)[:, None]
    n_rows = n_blocks * bm
    slots = jnp.zeros((n_rows,), I32).at[dest.reshape(-1)].set(slot_ids.reshape(-1))
    blk_row0 = jnp.arange(n_blocks, dtype=I32) * bm
    block_e = jnp.minimum(jnp.searchsorted(pends, blk_row0, side="right"), n_exp - 1).astype(I32)
    rows_left = lambda c: jnp.clip(pstarts[block_e] + c[block_e] - blk_row0, 0, bm).astype(I32)
    n_prompt = rows_left(counts_prompt)
    n_valid = rows_left(counts)
    n_active = (pends[-1] // bm).astype(I32).reshape(1)
    return slots.reshape(n_blocks, 1, bm), block_e, n_prompt, n_valid, n_active


def kernel(x_prompt, x_sample, state_pool, state_conv, state_lru_conv, state_lru_h, c_prompt, c_sample, ada_w, ada_b, norm1_g, w_in, pool_w, pool_scale, conv_dw, conv_b, conv_ln_g, conv_ln_b, lru_conv_w, lru_conv_b, lru_wa, lru_ba, lru_wx, lru_bx, lru_lambda, w_out, norm2_g, router_w, router_b, moe_w_gate, moe_b_gate, moe_w_up, moe_b_up, moe_w_down, moe_b_down, final_g):
    p = dict(norm1_g=norm1_g, w_in=w_in, pool_w=pool_w, pool_scale=pool_scale, conv_dw=conv_dw, conv_b=conv_b,
             conv_ln_g=conv_ln_g, conv_ln_b=conv_ln_b, lru_conv_w=lru_conv_w, lru_conv_b=lru_conv_b, lru_wa=lru_wa,
             lru_ba=lru_ba, lru_wx=lru_wx, lru_bx=lru_bx, lru_lambda=lru_lambda, w_out=w_out, norm2_g=norm2_g,
             router_w=router_w, router_b=router_b)
    n_b, seq, d_model = x_prompt.shape
    n_s = x_sample.shape[0]
    depth = ada_w.shape[0]
    n_exp = router_w.shape[2]
    t_p = n_b * seq
    t_all = t_p + n_s
    bm = MOE_BLOCK
    n_blocks = -(-(t_all * TOP_K) // bm) + n_exp

    mod = _adaln(jnp.concatenate([c_prompt, c_sample], axis=0), ada_w, ada_b)
    xp = x_prompt.reshape(t_p, d_model)
    xs = x_sample.reshape(n_s, d_model)
    tm = lambda a: jnp.swapaxes(a, 0, 1)

    new_p, new_s = [], []
    for l in range(depth):
        weights = _layer_weights(p, l)
        mod_p = mod[l, :n_b].reshape(n_b, 6, d_model)
        mod_s = mod[l, n_b:]
        x1p, h2p, ri_p, rw_p, cnt_p, npool, nconv, nlc, nlh = _mix_prompt(xp, mod_p, weights, n_b, seq)
        x1s, h2s, ri_s, rw_s, cnt, spool, sconv, slc, slh = _mix_sample(
            xs, mod_s, weights, tm(state_pool[l]), tm(state_conv[l]), tm(state_lru_conv[l]), state_lru_h[l],
            cnt_p, PAST_LEN)
        new_p.append((npool, nconv, nlc, nlh.reshape(n_b, -1)))
        new_s.append((tm(spool), tm(sconv), tm(slc), slh))

        ri = jnp.concatenate([ri_p, ri_s], axis=1)
        rw = jnp.concatenate([rw_p, rw_s], axis=1)
        slots, block_e, n_prompt, n_valid, n_active = _routing_plan(
            ri[:TOP_K], ri[TOP_K:], cnt_p[:, 0].astype(I32), cnt[:, 0].astype(I32), bm, n_blocks)
        ys = _experts(h2p, h2s, slots, block_e, n_prompt, n_valid, n_active, moe_w_gate[l], moe_w_up[l],
                      moe_w_down[l], moe_b_gate[l], moe_b_up[l], moe_b_down[l])
        w_tok = rw[:TOP_K].T
        final = l == depth - 1
        g2_p = mod[l, :n_b, 5 * d_model:]
        g2_s = mod[l, n_b:, 5 * d_model:]
        xp = _combine(x1p, ys, w_tok[:t_p], g2_p, final_g, 0, seq, final)
        xs = _combine(x1s, ys, w_tok[t_p:], g2_s, final_g, t_p, 1, final)

    stack = lambda items, i: jnp.stack([it[i] for it in items])
    return (xp.reshape(n_b, seq, d_model), xs.reshape(n_s, 1, d_model),
            stack(new_p, 0), stack(new_p, 1), stack(new_p, 2), stack(new_p, 3),
            stack(new_s, 0), stack(new_s, 1), stack(new_s, 2), stack(new_s, 3))
```

```python
import functools

import jax
import jax.numpy as jnp
from jax import lax
from jax.experimental import pallas as pl
from jax.experimental.pallas import tpu as pltpu

F32 = jnp.float32
BF16 = jnp.bfloat16
I32 = jnp.int32

POOL_WINDOWS = (2, 4, 8, 16)
POOL_BUF = max(POOL_WINDOWS) - 1
CONV_WIDTH = 31
CONV_BUF = CONV_WIDTH - 1
LRU_CONV_WIDTH = 4
LRU_BUF = LRU_CONV_WIDTH - 1
LRU_C = 8.0
TOP_K = 4
SWIGLU_ALPHA = 1.702
SWIGLU_LIMIT = 7.0
RMS_EPS = 1e-6
LN_EPS = 1e-5
PAST_LEN = 16384

V7X_SUBLANES = 8
V7X_LANES = 128
V7X_VMEM_LIMIT_BYTES = 56 * 1024 * 1024

SEQ_TILE = 256
MOE_BLOCK = 256
HIST_POOL = 16
HIST_CONV = 32
HIST_LRU = 8


def _bf16_round(x):
    return x.astype(BF16).astype(F32)


def _rms_mod(x, g, scale, shift):
    ms = jnp.mean(x * x, axis=-1, keepdims=True)
    return x * lax.rsqrt(ms + RMS_EPS) * g * (1.0 + scale) + shift


def _pool_window_select(accs, lane, pool_group):
    n = len(POOL_WINDOWS)
    wsum = accs[POOL_WINDOWS[-1]]
    wlen = jnp.full(lane.shape, POOL_WINDOWS[-1], I32)
    for g in reversed(range(n - 1)):
        m = lane < (g + 1) * pool_group
        wsum = jnp.where(m, accs[POOL_WINDOWS[g]], wsum)
        wlen = jnp.where(m, POOL_WINDOWS[g], wlen)
    return wsum, wlen


def _layer_norm_silu(y, g, b):
    mu = jnp.mean(y, axis=-1, keepdims=True)
    yc = y - mu
    var = jnp.mean(yc * yc, axis=-1, keepdims=True)
    return jax.nn.silu(yc * lax.rsqrt(var + LN_EPS) * g + b)


def _softplus(x):
    return jnp.maximum(x, 0.0) + jnp.log1p(jnp.exp(-jnp.abs(x)))


def _lru_gates(xc, wa_ref, ba, wx_ref, bx, lam, reset):
    half = xc.shape[1] // 2
    xb = xc.astype(BF16)

    def heads(w_ref):
        lo = jnp.dot(xb[:, :half], w_ref[0], preferred_element_type=F32)
        hi = jnp.dot(xb[:, half:], w_ref[1], preferred_element_type=F32)
        return jnp.concatenate([lo, hi], axis=1)

    r = jax.nn.sigmoid(heads(wa_ref) + ba)
    i = jax.nn.sigmoid(heads(wx_ref) + bx)
    log_a = -LRU_C * r * _softplus(-lam)
    a = jnp.exp(log_a)
    mult = jnp.sqrt(1.0 - a * a)
    if reset is not None:
        a = jnp.where(reset, 0.0, a)
        mult = jnp.where(reset, 1.0, mult)
    return a, mult * i * xc


def _scan_rows(a, b, h0):
    rows, c = a.shape
    groups = rows // V7X_SUBLANES
    a3 = a.reshape(groups, V7X_SUBLANES, c)
    b3 = b.reshape(groups, V7X_SUBLANES, c)
    sub = lax.broadcasted_iota(I32, a3.shape, 1)
    d = 1
    while d < V7X_SUBLANES:
        ar = pltpu.roll(a3, d, axis=1)
        br = pltpu.roll(b3, d, axis=1)
        m = sub >= d
        b3 = jnp.where(m, a3 * br + b3, b3)
        a3 = jnp.where(m, a3 * ar, a3)
        d *= 2
    hp = h0
    outs = []
    for g in range(groups):
        hg = a3[g] * hp + b3[g]
        outs.append(hg)
        hp = hg[V7X_SUBLANES - 1:V7X_SUBLANES, :]
    return jnp.concatenate(outs, axis=0)


def _to_token_tiles(ref, x):
    n, d = x.shape
    for s in range(d // V7X_LANES):
        ref[pl.ds(s, n, stride=V7X_SUBLANES), :] = x[:, s * V7X_LANES:(s + 1) * V7X_LANES]


def _from_token_tiles(ref, n):
    return jnp.concatenate([ref[pl.ds(s, n, stride=V7X_SUBLANES), :] for s in range(V7X_SUBLANES)], axis=1)


def _route(h2, rwt_ref, rb_ref, xl_ref, rt_ref, cnt_ref):
    n = h2.shape[0]
    n_exp = rwt_ref.shape[0]
    logits = lax.dot_general(rwt_ref[...], h2.astype(BF16), (((1,), (1,)), ((), ())),
                             preferred_element_type=F32) + rb_ref[...]
    eio = lax.broadcasted_iota(I32, (n_exp, n), 0).astype(F32)
    l = logits
    vals, sels = [], []
    for _ in range(TOP_K):
        m = jnp.max(l, axis=0, keepdims=True)
        idx = jnp.min(jnp.where(l == m, eio, float(n_exp)), axis=0, keepdims=True)
        sel = eio == idx
        vals.append(m)
        sels.append(sel)
        l = jnp.where(sel, -jnp.inf, l)
    exps = [jnp.exp(v - vals[0]) for v in vals]
    den = exps[0]
    for e in exps[1:]:
        den = den + e
    ws = [e / den for e in exps]
    onehot = jnp.zeros((n_exp, n), F32)
    for sel in sels:
        onehot = jnp.where(sel, 1.0, onehot)
    earlier = lax.broadcasted_iota(I32, (n, n), 0) < lax.broadcasted_iota(I32, (n, n), 1)
    prior = jnp.dot(onehot.astype(BF16), jnp.where(earlier, 1.0, 0.0).astype(BF16), preferred_element_type=F32)
    cnt = jnp.broadcast_to(jnp.sum(onehot, axis=1, keepdims=True), (n_exp, V7X_LANES))
    lower = lax.broadcasted_iota(I32, (n_exp, n_exp), 0) > lax.broadcasted_iota(I32, (n_exp, n_exp), 1)
    off = jnp.dot(jnp.where(lower, 1.0, 0.0), cnt, precision=lax.Precision.HIGHEST, preferred_element_type=F32)
    pos = prior + jnp.concatenate([off] * (n // V7X_LANES), axis=1)
    rhos = [jnp.sum(jnp.where(sel, pos, 0.0), axis=0, keepdims=True) for sel in sels]
    rt_ref[...] = jnp.concatenate(rhos + ws, axis=0)
    cnt_ref[...] = cnt
    rows = lax.broadcasted_iota(I32, (TOP_K * n, n), 0).astype(F32)
    perm = jnp.zeros((TOP_K * n, n), F32)
    for rho in rhos:
        perm = jnp.where(rows == rho, 1.0, perm)
    xl = jnp.dot(perm.astype(BF16), h2.astype(BF16), preferred_element_type=F32)
    _to_token_tiles(xl_ref, xl)


def _mix_prompt_kernel(x_ref, mod_ref, n1g_ref, win_ref, poolw_ref, pools_ref, cdw_ref, cb_ref, lng_ref, lnb_ref,
                       lcw_ref, lcb_ref, wa_ref, ba_ref, wx_ref, bx_ref, lam_ref, wout_ref, n2g_ref, rwt_ref,
                       rb_ref,
                       x1_ref, xl_ref, rt_ref, cnt_ref, npool_ref, nconv_ref, nlc_ref, nlh_ref,
                       e_s, a_s, xe_s, h_s, *, ts, d_pool, d_conv, d_lru):
    s = pl.program_id(1)

    @pl.when(s == 0)
    def _():
        e_s[0:HIST_POOL, :] = jnp.zeros((HIST_POOL, d_pool), F32)
        a_s[0:HIST_CONV, :] = jnp.zeros((HIST_CONV, d_conv), F32)
        xe_s[0:HIST_LRU, :] = jnp.zeros((HIST_LRU, d_lru), F32)
        h_s[...] = jnp.zeros(h_s.shape, F32)

    x = x_ref[...]
    m = mod_ref[0]
    sh1, sc1, g1, sh2, sc2, g2 = [m[i:i + 1] for i in range(6)]
    h = _rms_mod(x, n1g_ref[...], sc1, sh1)
    z = jnp.dot(h.astype(BF16), win_ref[...], preferred_element_type=F32)
    o0 = 0
    u = z[:, o0:o0 + d_pool]
    o0 += d_pool
    v = z[:, o0:o0 + d_conv]
    o0 += d_conv
    gc = z[:, o0:o0 + d_conv]
    o0 += d_conv
    xr = z[:, o0:o0 + d_lru]
    o0 += d_lru
    gl = z[:, o0:o0 + d_lru]

    e_s[HIST_POOL:HIST_POOL + ts, :] = u
    acc = u
    accs = {}
    for j in range(1, POOL_BUF + 1):
        acc = acc + e_s[pl.ds(HIST_POOL - j, ts), :]
        if j + 1 in POOL_WINDOWS:
            accs[j + 1] = acc
    lane = lax.broadcasted_iota(I32, (ts, d_pool), 1)
    pos = s * ts + lax.broadcasted_iota(I32, (ts, d_pool), 0)
    wsum, wlen = _pool_window_select(accs, lane, d_pool // len(POOL_WINDOWS))
    cnt = jnp.minimum(pos + 1, wlen).astype(F32)
    dmean = wsum / cnt - u
    o_pool = jnp.dot(dmean.astype(BF16), poolw_ref[...], preferred_element_type=F32) * pools_ref[...]
    npool_ref[0] = e_s[pl.ds(HIST_POOL + ts - POOL_BUF, POOL_BUF), :]
    e_s[0:HIST_POOL, :] = e_s[pl.ds(ts, HIST_POOL), :]

    a = v * jax.nn.sigmoid(gc)
    a_s[HIST_CONV:HIST_CONV + ts, :] = _bf16_round(a)
    y = jnp.broadcast_to(cb_ref[...], (ts, d_conv))
    for j in range(CONV_WIDTH):
        y = y + _bf16_round(cdw_ref[j:j + 1, :]) * a_s[pl.ds(HIST_CONV - CONV_BUF + j, ts), :]
    o_conv = _layer_norm_silu(y, lng_ref[...], lnb_ref[...])
    nconv_ref[0] = a[ts - CONV_BUF:ts, :]
    a_s[0:HIST_CONV, :] = a_s[pl.ds(ts, HIST_CONV), :]

    xe_s[HIST_LRU:HIST_LRU + ts, :] = _bf16_round(xr)
    xc = jnp.broadcast_to(lcb_ref[...], (ts, d_lru))
    for j in range(LRU_CONV_WIDTH):
        xc = xc + _bf16_round(lcw_ref[j:j + 1, :]) * xe_s[pl.ds(HIST_LRU - LRU_BUF + j, ts), :]
    reset = (s * ts + lax.broadcasted_iota(I32, (ts, d_lru), 0)) == 0
    a_t, b_t = _lru_gates(xc, wa_ref, ba_ref[...], wx_ref, bx_ref[...], lam_ref[...], reset)
    hseq = _scan_rows(a_t, b_t, h_s[0:1, :])
    o_lru = hseq * jax.nn.gelu(gl)
    nlc_ref[0] = xr[ts - LRU_BUF:ts, :]
    nlh_ref[0] = hseq[ts - 1:ts, :]
    h_s[...] = jnp.broadcast_to(hseq[ts - 1:ts, :], h_s.shape)
    xe_s[0:HIST_LRU, :] = xe_s[pl.ds(ts, HIST_LRU), :]

    o = jnp.concatenate([o_pool, o_conv, o_lru], axis=1).astype(BF16)
    x1 = x + g1 * jnp.dot(o, wout_ref[...], preferred_element_type=F32)
    x1_ref[...] = x1
    h2 = _rms_mod(x1, n2g_ref[...], sc2, sh2)
    _route(h2, rwt_ref, rb_ref, xl_ref, rt_ref, cnt_ref)


def _mix_sample_kernel(x_ref, mod_ref, n1g_ref, win_ref, poolw_ref, pools_ref, cdw_ref, cb_ref, lng_ref, lnb_ref,
                       lcw_ref, lcb_ref, wa_ref, ba_ref, wx_ref, bx_ref, lam_ref, wout_ref, n2g_ref, rwt_ref,
                       rb_ref, sp_ref, sc_ref, sl_ref, sh_ref,
                       x1_ref, xl_ref, rt_ref, cnt_ref, npool_ref, nconv_ref, nlc_ref, nlh_ref,
                       *, start, d_model, d_pool, d_conv, d_lru):
    n = x_ref.shape[0]
    x = x_ref[...]
    m = mod_ref[...]
    sh1, sc1, g1, sh2, sc2, g2 = [m[:, i * d_model:(i + 1) * d_model] for i in range(6)]
    h = _rms_mod(x, n1g_ref[...], sc1, sh1)
    z = jnp.dot(h.astype(BF16), win_ref[...], preferred_element_type=F32)
    o0 = 0
    u = z[:, o0:o0 + d_pool]
    o0 += d_pool
    v = z[:, o0:o0 + d_conv]
    o0 += d_conv
    gc = z[:, o0:o0 + d_conv]
    o0 += d_conv
    xr = z[:, o0:o0 + d_lru]
    o0 += d_lru
    gl = z[:, o0:o0 + d_lru]

    acc = u
    accs = {}
    for j in range(1, POOL_BUF + 1):
        acc = acc + sp_ref[POOL_BUF - j]
        if j + 1 in POOL_WINDOWS:
            accs[j + 1] = acc
    lane = lax.broadcasted_iota(I32, (n, d_pool), 1)
    wsum, wlen = _pool_window_select(accs, lane, d_pool // len(POOL_WINDOWS))
    cnt = jnp.minimum(start + 1, wlen).astype(F32)
    dmean = wsum / cnt - u
    o_pool = jnp.dot(dmean.astype(BF16), poolw_ref[...], preferred_element_type=F32) * pools_ref[...]
    for j in range(POOL_BUF - 1):
        npool_ref[j] = sp_ref[j + 1]
    npool_ref[POOL_BUF - 1] = u

    a = v * jax.nn.sigmoid(gc)
    y = cb_ref[...] + _bf16_round(cdw_ref[CONV_BUF:CONV_WIDTH, :]) * _bf16_round(a)
    for j in range(CONV_BUF):
        y = y + _bf16_round(cdw_ref[j:j + 1, :]) * _bf16_round(sc_ref[j])
    o_conv = _layer_norm_silu(y, lng_ref[...], lnb_ref[...])
    for j in range(CONV_BUF - 1):
        nconv_ref[j] = sc_ref[j + 1]
    nconv_ref[CONV_BUF - 1] = a

    xc = lcb_ref[...] + _bf16_round(lcw_ref[LRU_BUF:LRU_CONV_WIDTH, :]) * _bf16_round(xr)
    for j in range(LRU_BUF):
        xc = xc + _bf16_round(lcw_ref[j:j + 1, :]) * _bf16_round(sl_ref[j])
    reset = jnp.full((n, d_lru), True) if start == 0 else None
    a_t, b_t = _lru_gates(xc, wa_ref, ba_ref[...], wx_ref, bx_ref[...], lam_ref[...], reset)
    hnew = a_t * sh_ref[...] + b_t
    o_lru = hnew * jax.nn.gelu(gl)
    for j in range(LRU_BUF - 1):
        nlc_ref[j] = sl_ref[j + 1]
    nlc_ref[LRU_BUF - 1] = xr
    nlh_ref[...] = hnew

    o = jnp.concatenate([o_pool, o_conv, o_lru], axis=1).astype(BF16)
    x1 = x + g1 * jnp.dot(o, wout_ref[...], preferred_element_type=F32)
    x1_ref[...] = x1
    h2 = _rms_mod(x1, n2g_ref[...], sc2, sh2)
    _route(h2, rwt_ref, rb_ref, xl_ref, rt_ref, cnt_ref)


def _adaln_kernel(c_ref, w_ref, b_ref, o_ref):
    c = c_ref[...]
    o_ref[0] = jnp.dot(jax.nn.silu(c).astype(BF16), w_ref[0].astype(BF16), preferred_element_type=F32) + b_ref[0]


def _adaln(c_all, ada_w, ada_b):
    depth, d, n6 = ada_w.shape
    nb = c_all.shape[0]
    tn = d
    return pl.pallas_call(
        _adaln_kernel,
        grid=(depth, n6 // tn),
        in_specs=[pl.BlockSpec((nb, d), lambda l, j: (0, 0)),
                  pl.BlockSpec((1, d, tn), lambda l, j: (l, 0, j)),
                  pl.BlockSpec((1, 1, tn), lambda l, j: (l, 0, j))],
        out_specs=pl.BlockSpec((1, nb, tn), lambda l, j: (l, 0, j)),
        out_shape=jax.ShapeDtypeStruct((depth, nb, n6), F32),
        compiler_params=pltpu.CompilerParams(dimension_semantics=("arbitrary", "arbitrary")),
        name="adaln",
    )(c_all, ada_w, ada_b.reshape(depth, 1, n6))


def _const_spec(shape):
    nd = len(shape)
    return pl.BlockSpec(shape, lambda *_: (0,) * nd)


def _block_diag(w):
    g, a, b = w.shape
    return jnp.einsum("gij,gh->gihj", w, jnp.eye(g, dtype=w.dtype)).reshape(g * a, g * b)


def _layer_weights(p, l):
    def head_tiles(w):
        nh = w.shape[0]
        return jnp.stack([_block_diag(w[:nh // 2]), _block_diag(w[nh // 2:])]).astype(BF16)

    row = lambda a: a[l].reshape(1, -1)
    return [row(p["norm1_g"]), p["w_in"][l].astype(BF16), _block_diag(p["pool_w"][l]).astype(BF16),
            row(p["pool_scale"]),
            p["conv_dw"][l], row(p["conv_b"]), row(p["conv_ln_g"]), row(p["conv_ln_b"]),
            p["lru_conv_w"][l], row(p["lru_conv_b"]), head_tiles(p["lru_wa"][l]), row(p["lru_ba"]),
            head_tiles(p["lru_wx"][l]), row(p["lru_bx"]), row(p["lru_lambda"]),
            p["w_out"][l].astype(BF16), row(p["norm2_g"]), p["router_w"][l].T.astype(BF16),
            p["router_b"][l].reshape(-1, 1)]


def _mix_prompt(x, mod, weights, n_batch, seq):
    t, d_model = x.shape
    ts = SEQ_TILE
    n_s = seq // ts
    d_pool = weights[3].shape[1]
    d_conv = weights[5].shape[1]
    d_lru = weights[9].shape[1]
    n_exp = weights[17].shape[0]
    tile_rows = TOP_K * ts * V7X_SUBLANES
    tok = lambda b, s: (b * n_s + s, 0)
    lanes = lambda b, s: (0, b * n_s + s)
    per_b = lambda b, s: (b, 0, 0)
    in_specs = [pl.BlockSpec((ts, d_model), tok), pl.BlockSpec((1, 6, d_model), per_b)]
    in_specs += [_const_spec(w.shape) for w in weights]
    out_shape = (jax.ShapeDtypeStruct((t, d_model), F32),
                 jax.ShapeDtypeStruct((t * TOP_K * V7X_SUBLANES, V7X_LANES), F32),
                 jax.ShapeDtypeStruct((2 * TOP_K, t), F32),
                 jax.ShapeDtypeStruct((t // ts * n_exp, V7X_LANES), F32),
                 jax.ShapeDtypeStruct((n_batch, POOL_BUF, d_pool), F32),
                 jax.ShapeDtypeStruct((n_batch, CONV_BUF, d_conv), F32),
                 jax.ShapeDtypeStruct((n_batch, LRU_BUF, d_lru), F32),
                 jax.ShapeDtypeStruct((n_batch, 1, d_lru), F32))
    out_specs = (pl.BlockSpec((ts, d_model), tok), pl.BlockSpec((tile_rows, V7X_LANES), tok),
                 pl.BlockSpec((2 * TOP_K, ts), lanes), pl.BlockSpec((n_exp, V7X_LANES), tok),
                 pl.BlockSpec((1, POOL_BUF, d_pool), per_b), pl.BlockSpec((1, CONV_BUF, d_conv), per_b),
                 pl.BlockSpec((1, LRU_BUF, d_lru), per_b), pl.BlockSpec((1, 1, d_lru), per_b))
    scratch = [pltpu.VMEM((HIST_POOL + ts, d_pool), F32), pltpu.VMEM((HIST_CONV + ts, d_conv), F32),
               pltpu.VMEM((HIST_LRU + ts, d_lru), F32), pltpu.VMEM((V7X_SUBLANES, d_lru), F32)]
    return pl.pallas_call(
        functools.partial(_mix_prompt_kernel, ts=ts, d_pool=d_pool, d_conv=d_conv, d_lru=d_lru),
        grid=(n_batch, n_s), in_specs=in_specs, out_specs=out_specs, out_shape=out_shape, scratch_shapes=scratch,
        compiler_params=pltpu.CompilerParams(dimension_semantics=("arbitrary", "arbitrary"),
                                             vmem_limit_bytes=V7X_VMEM_LIMIT_BYTES),
        name="mix_prompt",
    )(x, mod, *weights)


def _mix_sample(x, mod, weights, st_pool, st_conv, st_lruc, st_lruh, start):
    n, d_model = x.shape
    d_pool = weights[3].shape[1]
    d_conv = weights[5].shape[1]
    d_lru = weights[9].shape[1]
    n_exp = weights[17].shape[0]
    ins = [x, mod] + list(weights) + [st_pool, st_conv, st_lruc, st_lruh]
    in_specs = [_const_spec(a.shape) for a in ins]
    shapes = [(n, d_model), (n * TOP_K * V7X_SUBLANES, V7X_LANES), (2 * TOP_K, n), (n_exp, V7X_LANES),
              st_pool.shape, st_conv.shape, st_lruc.shape, st_lruh.shape]
    return pl.pallas_call(
        functools.partial(_mix_sample_kernel, start=start, d_model=d_model, d_pool=d_pool, d_conv=d_conv,
                          d_lru=d_lru),
        grid=(1,), in_specs=in_specs, out_specs=tuple(_const_spec(s) for s in shapes),
        out_shape=tuple(jax.ShapeDtypeStruct(s, F32) for s in shapes),
        compiler_params=pltpu.CompilerParams(dimension_semantics=("arbitrary",),
                                             vmem_limit_bytes=V7X_VMEM_LIMIT_BYTES),
        name="mix_sample",
    )(*ins)


def _rows(ref, row0, n_rows):
    start = pl.multiple_of(row0 * V7X_SUBLANES, V7X_SUBLANES)
    size = pl.multiple_of(n_rows * V7X_SUBLANES, V7X_SUBLANES)
    return ref.at[pl.ds(start, size)]


def _expert_kernel(be_ref, jlo_ref, jhi_ref, nv_ref, na_ref, gdst_ref, glen_ref, gsrc_ref,
                   xlp_ref, xls_ref, wg_ref, wu_ref, wd_ref, bg_ref, bu_ref, bd_ref, ys_ref,
                   xbuf, wbf, gsem, *, bm, runs_per_expert):
    i = pl.program_id(0)
    n_active = na_ref[0]
    cur = i % 2

    def start_gather(blk, buf_slot):
        blk0 = blk * bm
        run0 = be_ref[blk] * runs_per_expert

        def fetch(r, src_ref):
            g0 = gdst_ref[r]
            lo = jnp.maximum(g0, blk0)
            n = jnp.minimum(g0 + glen_ref[r], blk0 + bm) - lo

            @pl.when(n > 0)
            def _():
                pltpu.make_async_copy(_rows(src_ref, gsrc_ref[r] + lo - g0, n),
                                      _rows(xbuf.at[buf_slot], lo - blk0, n), gsem.at[buf_slot]).start()

        def body(j, c):
            fetch(run0 + j, xlp_ref)
            return c
        lax.fori_loop(jlo_ref[blk], jhi_ref[blk], body, 0)
        fetch(run0 + runs_per_expert - 1, xls_ref)

    def wait_gather(blk, buf_slot):
        @pl.when(nv_ref[blk] > 0)
        def _():
            rows = _rows(xbuf.at[buf_slot], 0, nv_ref[blk])
            pltpu.make_async_copy(rows, rows, gsem.at[buf_slot]).wait()

    @pl.when(i == 0)
    def _():
        xbuf[...] = jnp.zeros(xbuf.shape, F32)
        start_gather(0, 0)

    @pl.when(i + 1 < n_active)
    def _():
        start_gather(i + 1, 1 - cur)

    @pl.when(i < n_active)
    def _():
        changed = jnp.logical_or(i == 0, be_ref[i] != be_ref[jnp.maximum(i - 1, 0)])

        @pl.when(changed)
        def _():
            wbf[0] = wg_ref[0, 0].astype(BF16)
            wbf[1] = wu_ref[0, 0].astype(BF16)
            wbf[2] = wd_ref[0, 0].astype(BF16)

        wait_gather(i, cur)
        x = _from_token_tiles(xbuf.at[cur], bm).astype(BF16)
        g = jnp.dot(x, wbf[0], preferred_element_type=F32) + bg_ref[0, 0]
        u = jnp.dot(x, wbf[1], preferred_element_type=F32) + bu_ref[0, 0]
        g = jnp.minimum(g, SWIGLU_LIMIT)
        u = jnp.clip(u, -SWIGLU_LIMIT, SWIGLU_LIMIT)
        act = (u + 1.0) * (g * jax.nn.sigmoid(SWIGLU_ALPHA * g))
        y = jnp.dot(act.astype(BF16), wbf[2], preferred_element_type=F32) + bd_ref[0, 0]
        _to_token_tiles(ys_ref, y)

    @pl.when(i >= n_active)
    def _():
        ys_ref[...] = jnp.zeros(ys_ref.shape, F32)


def _experts(layer, xl_p, xl_s, plan, wg, wu, wd, bg, bu, bd, n_blocks, bm):
    _, n_exp, d, d_ff = wg.shape
    block_e, jlo, jhi, n_valid, n_active, gdst, glen, gsrc = plan
    runs_per_expert = gdst.shape[0] // n_exp
    w_blk = lambda i, be, *_: (layer, be[i], 0, 0)
    b3 = lambda b: b.reshape(b.shape[0], n_exp, 1, b.shape[-1])
    grid_spec = pltpu.PrefetchScalarGridSpec(
        num_scalar_prefetch=8, grid=(n_blocks,),
        in_specs=[pl.BlockSpec(memory_space=pl.ANY), pl.BlockSpec(memory_space=pl.ANY),
                  pl.BlockSpec((1, 1, d, d_ff), w_blk), pl.BlockSpec((1, 1, d, d_ff), w_blk),
                  pl.BlockSpec((1, 1, d_ff, d), w_blk),
                  pl.BlockSpec((1, 1, 1, d_ff), w_blk), pl.BlockSpec((1, 1, 1, d_ff), w_blk),
                  pl.BlockSpec((1, 1, 1, d), w_blk)],
        out_specs=pl.BlockSpec((bm * V7X_SUBLANES, V7X_LANES), lambda i, *_: (i, 0)),
        scratch_shapes=[pltpu.VMEM((2, bm * V7X_SUBLANES, V7X_LANES), F32), pltpu.VMEM((3, d, d_ff), BF16),
                        pltpu.SemaphoreType.DMA((2,))])
    return pl.pallas_call(
        functools.partial(_expert_kernel, bm=bm, runs_per_expert=runs_per_expert),
        grid_spec=grid_spec,
        out_shape=jax.ShapeDtypeStruct((n_blocks * bm * V7X_SUBLANES, V7X_LANES), F32),
        compiler_params=pltpu.CompilerParams(dimension_semantics=("arbitrary",),
                                             vmem_limit_bytes=V7X_VMEM_LIMIT_BYTES),
        name="experts",
    )(block_e, jlo, jhi, n_valid, n_active, gdst, glen, gsrc, xl_p, xl_s, wg, wu, wd, b3(bg), b3(bu), b3(bd))


def _combine_kernel(gdst_ref, glen_ref, off_ref, x1_ref, ys_ref, rt_ref, g2_ref, fg_ref, o_ref, ybuf, sem,
                    *, tile0, n_exp, runs_per_expert, final):
    step = pl.program_id(0)
    n_steps = pl.num_programs(0)
    n = x1_ref.shape[0]
    cur = step % 2

    def start_fetch(tile, buf_slot):
        def body(e, c):
            r = e * runs_per_expert + tile
            ln = glen_ref[r]

            @pl.when(ln > 0)
            def _():
                pltpu.make_async_copy(_rows(ys_ref, gdst_ref[r], ln), _rows(ybuf.at[buf_slot], off_ref[r], ln),
                                      sem.at[buf_slot]).start()
            return c
        lax.fori_loop(0, n_exp, body, 0)

    @pl.when(step == 0)
    def _():
        start_fetch(tile0, 0)

    @pl.when(step + 1 < n_steps)
    def _():
        start_fetch(tile0 + step + 1, 1 - cur)

    pltpu.make_async_copy(ybuf.at[cur], ybuf.at[cur], sem.at[cur]).wait()
    yl = _from_token_tiles(ybuf.at[cur], TOP_K * n)
    rt = rt_ref[...]
    col = lax.broadcasted_iota(I32, (n, TOP_K * n), 1).astype(F32)
    sel = jnp.zeros((n, TOP_K * n), F32)
    for k in range(TOP_K):
        sel = jnp.where(col == rt[:, k:k + 1], rt[:, TOP_K + k:TOP_K + k + 1], sel)
    y = jnp.dot(sel.astype(BF16), yl.astype(BF16), preferred_element_type=F32)
    x2 = x1_ref[...] + g2_ref[...] * y
    if final:
        ms = jnp.mean(x2 * x2, axis=-1, keepdims=True)
        x2 = x2 * lax.rsqrt(ms + RMS_EPS) * fg_ref[...]
    o_ref[...] = x2


def _combine(x1, ys, rt_tall, g2, final_g, tables, tile0, n_tile, rows_per_gate, final, n_exp):
    t, d = x1.shape
    gdst, glen, off = tables
    runs_per_expert = gdst.shape[0] // n_exp
    if rows_per_gate == 1:
        g2_spec = pl.BlockSpec((n_tile, d), lambda i, *_: (i, 0))
    else:
        g2 = g2.reshape(-1, 1, d)
        g2_spec = pl.BlockSpec((None, 1, d), lambda i, *_: ((i * n_tile) // rows_per_gate, 0, 0))
    grid_spec = pltpu.PrefetchScalarGridSpec(
        num_scalar_prefetch=3, grid=(t // n_tile,),
        in_specs=[pl.BlockSpec((n_tile, d), lambda i, *_: (i, 0)), pl.BlockSpec(memory_space=pl.ANY),
                  pl.BlockSpec((n_tile, 2 * TOP_K), lambda i, *_: (i, 0)), g2_spec,
                  pl.BlockSpec((1, d), lambda i, *_: (0, 0))],
        out_specs=pl.BlockSpec((n_tile, d), lambda i, *_: (i, 0)),
        scratch_shapes=[pltpu.VMEM((2, TOP_K * n_tile * V7X_SUBLANES, V7X_LANES), F32),
                        pltpu.SemaphoreType.DMA((2,))])
    return pl.pallas_call(
        functools.partial(_combine_kernel, tile0=tile0, n_exp=n_exp, runs_per_expert=runs_per_expert, final=final),
        grid_spec=grid_spec,
        out_shape=jax.ShapeDtypeStruct((t, d), F32),
        compiler_params=pltpu.CompilerParams(dimension_semantics=("arbitrary",),
                                             vmem_limit_bytes=V7X_VMEM_LIMIT_BYTES),
        name="combine",
    )(gdst, glen, off, x1, ys, rt_tall, g2, final_g.reshape(1, d))


def _routing_plan(tile_counts, tile_rows, bm, n_blocks):
    n_tiles, n_exp = tile_counts.shape
    counts = jnp.sum(tile_counts, axis=0)
    pcounts = (counts + bm - 1) // bm * bm
    pends = jnp.cumsum(pcounts)
    pstarts = pends - pcounts
    base = jnp.cumsum(tile_counts, axis=0) - tile_counts
    off = jnp.cumsum(tile_counts, axis=1) - tile_counts
    gdst = (pstarts[None, :] + base).T
    glen = tile_counts.T
    gsrc = (tile_rows[:, None] + off).T
    blk0 = jnp.arange(n_blocks, dtype=I32) * bm
    block_e = jnp.minimum(jnp.sum(pends[None, :] <= blk0[:, None], axis=1), n_exp - 1).astype(I32)
    pick = jax.nn.one_hot(block_e, n_exp, dtype=F32)
    hi = lax.Precision.HIGHEST
    gd_b = jnp.dot(pick, gdst.astype(F32), precision=hi)[:, :n_tiles - 1]
    ge_b = jnp.dot(pick, (gdst + glen).astype(F32), precision=hi)[:, :n_tiles - 1]
    blk0f = blk0.astype(F32)[:, None]
    jlo = jnp.sum(ge_b <= blk0f, axis=1).astype(I32)
    jhi = jnp.sum(gd_b < blk0f + bm, axis=1).astype(I32)
    n_valid = jnp.clip(pstarts[block_e] + counts[block_e] - blk0, 0, bm).astype(I32)
    n_active = (pends[-1] // bm).astype(I32).reshape(1)
    flat = lambda a: a.reshape(-1).astype(I32)
    return ((block_e, jlo, jhi, n_valid, n_active, flat(gdst), flat(glen), flat(gsrc)),
            (flat(gdst), flat(glen), flat(off.T)))


def kernel(x_prompt, x_sample, state_pool, state_conv, state_lru_conv, state_lru_h, c_prompt, c_sample, ada_w, ada_b, norm1_g, w_in, pool_w, pool_scale, conv_dw, conv_b, conv_ln_g, conv_ln_b, lru_conv_w, lru_conv_b, lru_wa, lru_ba, lru_wx, lru_bx, lru_lambda, w_out, norm2_g, router_w, router_b, moe_w_gate, moe_b_gate, moe_w_up, moe_b_up, moe_w_down, moe_b_down, final_g):
    p = dict(norm1_g=norm1_g, w_in=w_in, pool_w=pool_w, pool_scale=pool_scale, conv_dw=conv_dw, conv_b=conv_b,
             conv_ln_g=conv_ln_g, conv_ln_b=conv_ln_b, lru_conv_w=lru_conv_w, lru_conv_b=lru_conv_b, lru_wa=lru_wa,
             lru_ba=lru_ba, lru_wx=lru_wx, lru_bx=lru_bx, lru_lambda=lru_lambda, w_out=w_out, norm2_g=norm2_g,
             router_w=router_w, router_b=router_b)
    n_b, seq, d_model = x_prompt.shape
    n_s = x_sample.shape[0]
    depth = ada_w.shape[0]
    n_exp = router_w.shape[2]
    t_p = n_b * seq
    t_all = t_p + n_s
    ts = SEQ_TILE
    n_tiles_p = t_p // ts
    bm = MOE_BLOCK
    n_blocks = -(-(t_all * TOP_K) // bm) + n_exp
    tile_rows = jnp.concatenate([jnp.arange(n_tiles_p, dtype=I32) * (TOP_K * ts), jnp.zeros((1,), I32)])

    mod = _adaln(jnp.concatenate([c_prompt, c_sample], axis=0), ada_w, ada_b)
    xp = x_prompt.reshape(t_p, d_model)
    xs = x_sample.reshape(n_s, d_model)
    tm = lambda a: jnp.swapaxes(a, 0, 1)

    new_p, new_s = [], []
    for l in range(depth):
        weights = _layer_weights(p, l)
        mod_p = mod[l, :n_b].reshape(n_b, 6, d_model)
        mod_s = mod[l, n_b:]
        x1p, xl_p, rt_p, cnt_p, npool, nconv, nlc, nlh = _mix_prompt(xp, mod_p, weights, n_b, seq)
        x1s, xl_s, rt_s, cnt_s, spool, sconv, slc, slh = _mix_sample(
            xs, mod_s, weights, tm(state_pool[l]), tm(state_conv[l]), tm(state_lru_conv[l]), state_lru_h[l],
            PAST_LEN)
        new_p.append((npool, nconv, nlc, nlh.reshape(n_b, -1)))
        new_s.append((tm(spool), tm(sconv), tm(slc), slh))

        tile_counts = jnp.concatenate([cnt_p[:, 0].reshape(n_tiles_p, n_exp), cnt_s[:, 0].reshape(1, n_exp)],
                                      axis=0).astype(I32)
        plan, tables = _routing_plan(tile_counts, tile_rows, bm, n_blocks)
        ys = _experts(l, xl_p, xl_s, plan, moe_w_gate, moe_w_up, moe_w_down, moe_b_gate, moe_b_up, moe_b_down,
                      n_blocks, bm)
        final = l == depth - 1
        g2_p = mod[l, :n_b, 5 * d_model:]
        g2_s = mod[l, n_b:, 5 * d_model:]
        xp = _combine(x1p, ys, rt_p.T, g2_p, final_g, tables, 0, ts, seq, final, n_exp)
        xs = _combine(x1s, ys, rt_s.T, g2_s, final_g, tables, n_tiles_p, n_s, 1, final, n_exp)

    stack = lambda items, i: jnp.stack([it[i] for it in items])
    return (xp.reshape(n_b, seq, d_model), xs.reshape(n_s, 1, d_model),
            stack(new_p, 0), stack(new_p, 1), stack(new_p, 2), stack(new_p, 3),
            stack(new_s, 0), stack(new_s, 1), stack(new_s, 2), stack(new_s, 3))
```

```python
import functools

import jax
import jax.numpy as jnp
from jax import lax
from jax.experimental import pallas as pl
from jax.experimental.pallas import tpu as pltpu

F32 = jnp.float32
BF16 = jnp.bfloat16
I32 = jnp.int32

POOL_WINDOWS = (2, 4, 8, 16)
POOL_BUF = max(POOL_WINDOWS) - 1
CONV_WIDTH = 31
CONV_BUF = CONV_WIDTH - 1
LRU_CONV_WIDTH = 4
LRU_BUF = LRU_CONV_WIDTH - 1
LRU_C = 8.0
TOP_K = 4
SWIGLU_ALPHA = 1.702
SWIGLU_LIMIT = 7.0
RMS_EPS = 1e-6
LN_EPS = 1e-5
PAST_LEN = 16384

V7X_SUBLANES = 8
V7X_LANES = 128
V7X_VMEM_LIMIT_BYTES = 56 * 1024 * 1024

SEQ_TILE = 256
MOE_BLOCK = 512
HIST_POOL = 16
HIST_CONV = 32
HIST_LRU = 8


def _bf16_round(x):
    return x.astype(BF16).astype(F32)


def _rms_mod(x, g, scale, shift):
    ms = jnp.mean(x * x, axis=-1, keepdims=True)
    return x * lax.rsqrt(ms + RMS_EPS) * g * (1.0 + scale) + shift


def _pool_window_select(accs, lane, pool_group):
    n = len(POOL_WINDOWS)
    wsum = accs[POOL_WINDOWS[-1]]
    wlen = jnp.full(lane.shape, POOL_WINDOWS[-1], I32)
    for g in reversed(range(n - 1)):
        m = lane < (g + 1) * pool_group
        wsum = jnp.where(m, accs[POOL_WINDOWS[g]], wsum)
        wlen = jnp.where(m, POOL_WINDOWS[g], wlen)
    return wsum, wlen


def _layer_norm_silu(y, g, b):
    mu = jnp.mean(y, axis=-1, keepdims=True)
    yc = y - mu
    var = jnp.mean(yc * yc, axis=-1, keepdims=True)
    return jax.nn.silu(yc * lax.rsqrt(var + LN_EPS) * g + b)


def _softplus(x):
    return jnp.maximum(x, 0.0) + jnp.log1p(jnp.exp(-jnp.abs(x)))


def _lru_gates(xc, wa_ref, ba, wx_ref, bx, lam, reset):
    half = xc.shape[1] // 2
    xb = xc.astype(BF16)

    def heads(w_ref):
        lo = jnp.dot(xb[:, :half], w_ref[0], preferred_element_type=F32)
        hi = jnp.dot(xb[:, half:], w_ref[1], preferred_element_type=F32)
        return jnp.concatenate([lo, hi], axis=1)

    r = jax.nn.sigmoid(heads(wa_ref) + ba)
    i = jax.nn.sigmoid(heads(wx_ref) + bx)
    log_a = -LRU_C * r * _softplus(-lam)
    a = jnp.exp(log_a)
    mult = jnp.sqrt(1.0 - a * a)
    if reset is not None:
        a = jnp.where(reset, 0.0, a)
        mult = jnp.where(reset, 1.0, mult)
    return a, mult * i * xc


def _scan_rows(a, b, h0):
    rows, c = a.shape
    groups = rows // V7X_SUBLANES
    a3 = a.reshape(groups, V7X_SUBLANES, c)
    b3 = b.reshape(groups, V7X_SUBLANES, c)
    sub = lax.broadcasted_iota(I32, a3.shape, 1)
    d = 1
    while d < V7X_SUBLANES:
        ar = pltpu.roll(a3, d, axis=1)
        br = pltpu.roll(b3, d, axis=1)
        m = sub >= d
        b3 = jnp.where(m, a3 * br + b3, b3)
        a3 = jnp.where(m, a3 * ar, a3)
        d *= 2
    hp = h0
    outs = []
    for g in range(groups):
        hg = a3[g] * hp + b3[g]
        outs.append(hg)
        hp = hg[V7X_SUBLANES - 1:V7X_SUBLANES, :]
    return jnp.concatenate(outs, axis=0)


def _to_token_tiles(ref, x):
    n, d = x.shape
    for s in range(d // V7X_LANES):
        ref[pl.ds(s, n, stride=V7X_SUBLANES), :] = x[:, s * V7X_LANES:(s + 1) * V7X_LANES]


def _from_token_tiles(ref, n):
    return jnp.concatenate([ref[pl.ds(s, n, stride=V7X_SUBLANES), :] for s in range(V7X_SUBLANES)], axis=1)


def _route(h2, rwt_ref, rb_ref, xl_ref, rt_ref, cnt_ref):
    n = h2.shape[0]
    n_exp = rwt_ref.shape[0]
    logits = lax.dot_general(rwt_ref[...], h2.astype(BF16), (((1,), (1,)), ((), ())),
                             preferred_element_type=F32) + rb_ref[...]
    eio = lax.broadcasted_iota(I32, (n_exp, n), 0).astype(F32)
    l = logits
    vals, sels = [], []
    for _ in range(TOP_K):
        m = jnp.max(l, axis=0, keepdims=True)
        idx = jnp.min(jnp.where(l == m, eio, float(n_exp)), axis=0, keepdims=True)
        sel = eio == idx
        vals.append(m)
        sels.append(sel)
        l = jnp.where(sel, -jnp.inf, l)
    exps = [jnp.exp(v - vals[0]) for v in vals]
    den = exps[0]
    for e in exps[1:]:
        den = den + e
    ws = [e / den for e in exps]
    onehot = jnp.zeros((n_exp, n), F32)
    for sel in sels:
        onehot = jnp.where(sel, 1.0, onehot)
    earlier = lax.broadcasted_iota(I32, (n, n), 0) < lax.broadcasted_iota(I32, (n, n), 1)
    prior = jnp.dot(onehot.astype(BF16), jnp.where(earlier, 1.0, 0.0).astype(BF16), preferred_element_type=F32)
    cnt = jnp.broadcast_to(jnp.sum(onehot, axis=1, keepdims=True), (n_exp, V7X_LANES))
    lower = lax.broadcasted_iota(I32, (n_exp, n_exp), 0) > lax.broadcasted_iota(I32, (n_exp, n_exp), 1)
    off = jnp.dot(jnp.where(lower, 1.0, 0.0), cnt, precision=lax.Precision.HIGHEST, preferred_element_type=F32)
    pos = prior + jnp.concatenate([off] * (n // V7X_LANES), axis=1)
    rhos = [jnp.sum(jnp.where(sel, pos, 0.0), axis=0, keepdims=True) for sel in sels]
    rt_ref[...] = jnp.concatenate(rhos + ws, axis=0)
    cnt_ref[...] = cnt
    rows = lax.broadcasted_iota(I32, (TOP_K * n, n), 0).astype(F32)
    perm = jnp.zeros((TOP_K * n, n), F32)
    for rho in rhos:
        perm = jnp.where(rows == rho, 1.0, perm)
    xl = jnp.dot(perm.astype(BF16), h2.astype(BF16), preferred_element_type=F32)
    _to_token_tiles(xl_ref, xl)


def _mix_prompt_kernel(x_ref, mod_ref, n1g_ref, win_ref, poolw_ref, pools_ref, cdw_ref, cb_ref, lng_ref, lnb_ref,
                       lcw_ref, lcb_ref, wa_ref, ba_ref, wx_ref, bx_ref, lam_ref, wout_ref, n2g_ref, rwt_ref,
                       rb_ref,
                       x1_ref, xl_ref, rt_ref, cnt_ref, npool_ref, nconv_ref, nlc_ref, nlh_ref,
                       e_s, a_s, xe_s, h_s, *, ts, d_pool, d_conv, d_lru):
    s = pl.program_id(1)

    @pl.when(s == 0)
    def _():
        e_s[0:HIST_POOL, :] = jnp.zeros((HIST_POOL, d_pool), F32)
        a_s[0:HIST_CONV, :] = jnp.zeros((HIST_CONV, d_conv), F32)
        xe_s[0:HIST_LRU, :] = jnp.zeros((HIST_LRU, d_lru), F32)
        h_s[...] = jnp.zeros(h_s.shape, F32)

    x = x_ref[...]
    m = mod_ref[0]
    sh1, sc1, g1, sh2, sc2, g2 = [m[i:i + 1] for i in range(6)]
    h = _rms_mod(x, n1g_ref[...], sc1, sh1)
    z = jnp.dot(h.astype(BF16), win_ref[...], preferred_element_type=F32)
    o0 = 0
    u = z[:, o0:o0 + d_pool]
    o0 += d_pool
    v = z[:, o0:o0 + d_conv]
    o0 += d_conv
    gc = z[:, o0:o0 + d_conv]
    o0 += d_conv
    xr = z[:, o0:o0 + d_lru]
    o0 += d_lru
    gl = z[:, o0:o0 + d_lru]

    e_s[HIST_POOL:HIST_POOL + ts, :] = u
    acc = u
    accs = {}
    for j in range(1, POOL_BUF + 1):
        acc = acc + e_s[pl.ds(HIST_POOL - j, ts), :]
        if j + 1 in POOL_WINDOWS:
            accs[j + 1] = acc
    lane = lax.broadcasted_iota(I32, (ts, d_pool), 1)
    pos = s * ts + lax.broadcasted_iota(I32, (ts, d_pool), 0)
    wsum, wlen = _pool_window_select(accs, lane, d_pool // len(POOL_WINDOWS))
    cnt = jnp.minimum(pos + 1, wlen).astype(F32)
    dmean = wsum / cnt - u
    o_pool = jnp.dot(dmean.astype(BF16), poolw_ref[...], preferred_element_type=F32) * pools_ref[...]
    npool_ref[0] = e_s[pl.ds(HIST_POOL + ts - POOL_BUF, POOL_BUF), :]
    e_s[0:HIST_POOL, :] = e_s[pl.ds(ts, HIST_POOL), :]

    a = v * jax.nn.sigmoid(gc)
    a_s[HIST_CONV:HIST_CONV + ts, :] = _bf16_round(a)
    y = jnp.broadcast_to(cb_ref[...], (ts, d_conv))
    for j in range(CONV_WIDTH):
        y = y + _bf16_round(cdw_ref[j:j + 1, :]) * a_s[pl.ds(HIST_CONV - CONV_BUF + j, ts), :]
    o_conv = _layer_norm_silu(y, lng_ref[...], lnb_ref[...])
    nconv_ref[0] = a[ts - CONV_BUF:ts, :]
    a_s[0:HIST_CONV, :] = a_s[pl.ds(ts, HIST_CONV), :]

    xe_s[HIST_LRU:HIST_LRU + ts, :] = _bf16_round(xr)
    xc = jnp.broadcast_to(lcb_ref[...], (ts, d_lru))
    for j in range(LRU_CONV_WIDTH):
        xc = xc + _bf16_round(lcw_ref[j:j + 1, :]) * xe_s[pl.ds(HIST_LRU - LRU_BUF + j, ts), :]
    reset = (s * ts + lax.broadcasted_iota(I32, (ts, d_lru), 0)) == 0
    a_t, b_t = _lru_gates(xc, wa_ref, ba_ref[...], wx_ref, bx_ref[...], lam_ref[...], reset)
    hseq = _scan_rows(a_t, b_t, h_s[0:1, :])
    o_lru = hseq * jax.nn.gelu(gl)
    nlc_ref[0] = xr[ts - LRU_BUF:ts, :]
    nlh_ref[0] = hseq[ts - 1:ts, :]
    h_s[...] = jnp.broadcast_to(hseq[ts - 1:ts, :], h_s.shape)
    xe_s[0:HIST_LRU, :] = xe_s[pl.ds(ts, HIST_LRU), :]

    o = jnp.concatenate([o_pool, o_conv, o_lru], axis=1).astype(BF16)
    x1 = x + g1 * jnp.dot(o, wout_ref[...], preferred_element_type=F32)
    x1_ref[...] = x1
    h2 = _rms_mod(x1, n2g_ref[...], sc2, sh2)
    _route(h2, rwt_ref, rb_ref, xl_ref, rt_ref, cnt_ref)


def _mix_sample_kernel(x_ref, mod_ref, n1g_ref, win_ref, poolw_ref, pools_ref, cdw_ref, cb_ref, lng_ref, lnb_ref,
                       lcw_ref, lcb_ref, wa_ref, ba_ref, wx_ref, bx_ref, lam_ref, wout_ref, n2g_ref, rwt_ref,
                       rb_ref, sp_ref, sc_ref, sl_ref, sh_ref,
                       x1_ref, xl_ref, rt_ref, cnt_ref, npool_ref, nconv_ref, nlc_ref, nlh_ref,
                       *, start, d_model, d_pool, d_conv, d_lru):
    n = x_ref.shape[0]
    x = x_ref[...]
    m = mod_ref[...]
    sh1, sc1, g1, sh2, sc2, g2 = [m[:, i * d_model:(i + 1) * d_model] for i in range(6)]
    h = _rms_mod(x, n1g_ref[...], sc1, sh1)
    z = jnp.dot(h.astype(BF16), win_ref[...], preferred_element_type=F32)
    o0 = 0
    u = z[:, o0:o0 + d_pool]
    o0 += d_pool
    v = z[:, o0:o0 + d_conv]
    o0 += d_conv
    gc = z[:, o0:o0 + d_conv]
    o0 += d_conv
    xr = z[:, o0:o0 + d_lru]
    o0 += d_lru
    gl = z[:, o0:o0 + d_lru]

    acc = u
    accs = {}
    for j in range(1, POOL_BUF + 1):
        acc = acc + sp_ref[:, POOL_BUF - j, :]
        if j + 1 in POOL_WINDOWS:
            accs[j + 1] = acc
    lane = lax.broadcasted_iota(I32, (n, d_pool), 1)
    wsum, wlen = _pool_window_select(accs, lane, d_pool // len(POOL_WINDOWS))
    cnt = jnp.minimum(start + 1, wlen).astype(F32)
    dmean = wsum / cnt - u
    o_pool = jnp.dot(dmean.astype(BF16), poolw_ref[...], preferred_element_type=F32) * pools_ref[...]
    for j in range(POOL_BUF - 1):
        npool_ref[:, j, :] = sp_ref[:, j + 1, :]
    npool_ref[:, POOL_BUF - 1, :] = u

    a = v * jax.nn.sigmoid(gc)
    y = cb_ref[...] + _bf16_round(cdw_ref[CONV_BUF:CONV_WIDTH, :]) * _bf16_round(a)
    for j in range(CONV_BUF):
        y = y + _bf16_round(cdw_ref[j:j + 1, :]) * _bf16_round(sc_ref[:, j, :])
    o_conv = _layer_norm_silu(y, lng_ref[...], lnb_ref[...])
    for j in range(CONV_BUF - 1):
        nconv_ref[:, j, :] = sc_ref[:, j + 1, :]
    nconv_ref[:, CONV_BUF - 1, :] = a

    xc = lcb_ref[...] + _bf16_round(lcw_ref[LRU_BUF:LRU_CONV_WIDTH, :]) * _bf16_round(xr)
    for j in range(LRU_BUF):
        xc = xc + _bf16_round(lcw_ref[j:j + 1, :]) * _bf16_round(sl_ref[:, j, :])
    reset = jnp.full((n, d_lru), True) if start == 0 else None
    a_t, b_t = _lru_gates(xc, wa_ref, ba_ref[...], wx_ref, bx_ref[...], lam_ref[...], reset)
    hnew = a_t * sh_ref[...] + b_t
    o_lru = hnew * jax.nn.gelu(gl)
    for j in range(LRU_BUF - 1):
        nlc_ref[:, j, :] = sl_ref[:, j + 1, :]
    nlc_ref[:, LRU_BUF - 1, :] = xr
    nlh_ref[...] = hnew

    o = jnp.concatenate([o_pool, o_conv, o_lru], axis=1).astype(BF16)
    x1 = x + g1 * jnp.dot(o, wout_ref[...], preferred_element_type=F32)
    x1_ref[...] = x1
    h2 = _rms_mod(x1, n2g_ref[...], sc2, sh2)
    _route(h2, rwt_ref, rb_ref, xl_ref, rt_ref, cnt_ref)


def _adaln_kernel(c_ref, w_ref, b_ref, o_ref):
    c = c_ref[...]
    o_ref[0] = jnp.dot(jax.nn.silu(c).astype(BF16), w_ref[0].astype(BF16), preferred_element_type=F32) + b_ref[0]


def _adaln(c_all, ada_w, ada_b):
    depth, d, n6 = ada_w.shape
    nb = c_all.shape[0]
    tn = d
    return pl.pallas_call(
        _adaln_kernel,
        grid=(depth, n6 // tn),
        in_specs=[pl.BlockSpec((nb, d), lambda l, j: (0, 0)),
                  pl.BlockSpec((1, d, tn), lambda l, j: (l, 0, j)),
                  pl.BlockSpec((1, 1, tn), lambda l, j: (l, 0, j))],
        out_specs=pl.BlockSpec((1, nb, tn), lambda l, j: (l, 0, j)),
        out_shape=jax.ShapeDtypeStruct((depth, nb, n6), F32),
        compiler_params=pltpu.CompilerParams(dimension_semantics=("arbitrary", "arbitrary")),
        name="adaln",
    )(c_all, ada_w, ada_b.reshape(depth, 1, n6))


def _const_spec(shape):
    nd = len(shape)
    return pl.BlockSpec(shape, lambda *_: (0,) * nd)


def _block_diag(w):
    g, a, b = w.shape
    return jnp.einsum("gij,gh->gihj", w, jnp.eye(g, dtype=w.dtype)).reshape(g * a, g * b)


def _layer_weights(p, l):
    def head_tiles(w):
        nh = w.shape[0]
        return jnp.stack([_block_diag(w[:nh // 2]), _block_diag(w[nh // 2:])]).astype(BF16)

    row = lambda a: a[l].reshape(1, -1)
    return [row(p["norm1_g"]), p["w_in"][l].astype(BF16), _block_diag(p["pool_w"][l]).astype(BF16),
            row(p["pool_scale"]),
            p["conv_dw"][l], row(p["conv_b"]), row(p["conv_ln_g"]), row(p["conv_ln_b"]),
            p["lru_conv_w"][l], row(p["lru_conv_b"]), head_tiles(p["lru_wa"][l]), row(p["lru_ba"]),
            head_tiles(p["lru_wx"][l]), row(p["lru_bx"]), row(p["lru_lambda"]),
            p["w_out"][l].astype(BF16), row(p["norm2_g"]), p["router_w"][l].T.astype(BF16),
            p["router_b"][l].reshape(-1, 1)]


def _mix_prompt(x, mod, weights, n_batch, seq):
    t, d_model = x.shape
    ts = SEQ_TILE
    n_s = seq // ts
    d_pool = weights[3].shape[1]
    d_conv = weights[5].shape[1]
    d_lru = weights[9].shape[1]
    n_exp = weights[17].shape[0]
    tile_rows = TOP_K * ts * V7X_SUBLANES
    tok = lambda b, s: (b * n_s + s, 0)
    lanes = lambda b, s: (0, b * n_s + s)
    per_b = lambda b, s: (b, 0, 0)
    in_specs = [pl.BlockSpec((ts, d_model), tok), pl.BlockSpec((1, 6, d_model), per_b)]
    in_specs += [_const_spec(w.shape) for w in weights]
    out_shape = (jax.ShapeDtypeStruct((t, d_model), F32),
                 jax.ShapeDtypeStruct((t * TOP_K * V7X_SUBLANES, V7X_LANES), F32),
                 jax.ShapeDtypeStruct((2 * TOP_K, t), F32),
                 jax.ShapeDtypeStruct((t // ts * n_exp, V7X_LANES), F32),
                 jax.ShapeDtypeStruct((n_batch, POOL_BUF, d_pool), F32),
                 jax.ShapeDtypeStruct((n_batch, CONV_BUF, d_conv), F32),
                 jax.ShapeDtypeStruct((n_batch, LRU_BUF, d_lru), F32),
                 jax.ShapeDtypeStruct((n_batch, 1, d_lru), F32))
    out_specs = (pl.BlockSpec((ts, d_model), tok), pl.BlockSpec((tile_rows, V7X_LANES), tok),
                 pl.BlockSpec((2 * TOP_K, ts), lanes), pl.BlockSpec((n_exp, V7X_LANES), tok),
                 pl.BlockSpec((1, POOL_BUF, d_pool), per_b), pl.BlockSpec((1, CONV_BUF, d_conv), per_b),
                 pl.BlockSpec((1, LRU_BUF, d_lru), per_b), pl.BlockSpec((1, 1, d_lru), per_b))
    scratch = [pltpu.VMEM((HIST_POOL + ts, d_pool), F32), pltpu.VMEM((HIST_CONV + ts, d_conv), F32),
               pltpu.VMEM((HIST_LRU + ts, d_lru), F32), pltpu.VMEM((V7X_SUBLANES, d_lru), F32)]
    return pl.pallas_call(
        functools.partial(_mix_prompt_kernel, ts=ts, d_pool=d_pool, d_conv=d_conv, d_lru=d_lru),
        grid=(n_batch, n_s), in_specs=in_specs, out_specs=out_specs, out_shape=out_shape, scratch_shapes=scratch,
        compiler_params=pltpu.CompilerParams(dimension_semantics=("arbitrary", "arbitrary"),
                                             vmem_limit_bytes=V7X_VMEM_LIMIT_BYTES),
        name="mix_prompt",
    )(x, mod, *weights)


def _mix_sample(x, mod, weights, st_pool, st_conv, st_lruc, st_lruh, start):
    n, d_model = x.shape
    d_pool = weights[3].shape[1]
    d_conv = weights[5].shape[1]
    d_lru = weights[9].shape[1]
    n_exp = weights[17].shape[0]
    ins = [x, mod] + list(weights) + [st_pool, st_conv, st_lruc, st_lruh]
    in_specs = [_const_spec(a.shape) for a in ins]
    shapes = [(n, d_model), (n * TOP_K * V7X_SUBLANES, V7X_LANES), (2 * TOP_K, n), (n_exp, V7X_LANES),
              st_pool.shape, st_conv.shape, st_lruc.shape, st_lruh.shape]
    return pl.pallas_call(
        functools.partial(_mix_sample_kernel, start=start, d_model=d_model, d_pool=d_pool, d_conv=d_conv,
                          d_lru=d_lru),
        grid=(1,), in_specs=in_specs, out_specs=tuple(_const_spec(s) for s in shapes),
        out_shape=tuple(jax.ShapeDtypeStruct(s, F32) for s in shapes),
        compiler_params=pltpu.CompilerParams(dimension_semantics=("arbitrary",),
                                             vmem_limit_bytes=V7X_VMEM_LIMIT_BYTES),
        name="mix_sample",
    )(*ins)


def _rows(ref, row0, n_rows):
    start = pl.multiple_of(row0 * V7X_SUBLANES, V7X_SUBLANES)
    size = pl.multiple_of(n_rows * V7X_SUBLANES, V7X_SUBLANES)
    return ref.at[pl.ds(start, size)]


def _expert_kernel(be_ref, jlo_ref, jhi_ref, nv_ref, na_ref, gdst_ref, glen_ref, gsrc_ref,
                   xlp_ref, xls_ref, wg_ref, wu_ref, wd_ref, bg_ref, bu_ref, bd_ref, ys_ref,
                   xbuf, wbf, gsem, *, bm, runs_per_expert):
    i = pl.program_id(0)
    n_active = na_ref[0]
    cur = i % 2

    def start_gather(blk, buf_slot):
        blk0 = blk * bm
        run0 = be_ref[blk] * runs_per_expert

        def fetch(r, src_ref):
            g0 = gdst_ref[r]
            lo = jnp.maximum(g0, blk0)
            n = jnp.minimum(g0 + glen_ref[r], blk0 + bm) - lo

            @pl.when(n > 0)
            def _():
                pltpu.make_async_copy(_rows(src_ref, gsrc_ref[r] + lo - g0, n),
                                      _rows(xbuf.at[buf_slot], lo - blk0, n), gsem.at[buf_slot]).start()

        def body(j, c):
            fetch(run0 + j, xlp_ref)
            return c
        lax.fori_loop(jlo_ref[blk], jhi_ref[blk], body, 0)
        fetch(run0 + runs_per_expert - 1, xls_ref)

    def wait_gather(blk, buf_slot):
        @pl.when(nv_ref[blk] > 0)
        def _():
            rows = _rows(xbuf.at[buf_slot], 0, nv_ref[blk])
            pltpu.make_async_copy(rows, rows, gsem.at[buf_slot]).wait()

    @pl.when(i == 0)
    def _():
        xbuf[...] = jnp.zeros(xbuf.shape, F32)
        start_gather(0, 0)

    @pl.when(i + 1 < n_active)
    def _():
        start_gather(i + 1, 1 - cur)

    @pl.when(i < n_active)
    def _():
        changed = jnp.logical_or(i == 0, be_ref[i] != be_ref[jnp.maximum(i - 1, 0)])

        @pl.when(changed)
        def _():
            wbf[0] = wg_ref[0, 0].astype(BF16)
            wbf[1] = wu_ref[0, 0].astype(BF16)
            wbf[2] = wd_ref[0, 0].astype(BF16)

        wait_gather(i, cur)
        x = _from_token_tiles(xbuf.at[cur], bm).astype(BF16)
        g = jnp.dot(x, wbf[0], preferred_element_type=F32) + bg_ref[0, 0]
        u = jnp.dot(x, wbf[1], preferred_element_type=F32) + bu_ref[0, 0]
        g = jnp.minimum(g, SWIGLU_LIMIT)
        u = jnp.clip(u, -SWIGLU_LIMIT, SWIGLU_LIMIT)
        act = (u + 1.0) * (g * jax.nn.sigmoid(SWIGLU_ALPHA * g))
        y = jnp.dot(act.astype(BF16), wbf[2], preferred_element_type=F32) + bd_ref[0, 0]
        _to_token_tiles(ys_ref, y)

    @pl.when(i >= n_active)
    def _():
        ys_ref[...] = jnp.zeros(ys_ref.shape, F32)


def _experts(layer, xl_p, xl_s, plan, wg, wu, wd, bg, bu, bd, n_blocks, bm):
    _, n_exp, d, d_ff = wg.shape
    block_e, jlo, jhi, n_valid, n_active, gdst, glen, gsrc = plan
    runs_per_expert = gdst.shape[0] // n_exp
    w_blk = lambda i, be, *_: (layer, be[i], 0, 0)
    b3 = lambda b: b.reshape(b.shape[0], n_exp, 1, b.shape[-1])
    grid_spec = pltpu.PrefetchScalarGridSpec(
        num_scalar_prefetch=8, grid=(n_blocks,),
        in_specs=[pl.BlockSpec(memory_space=pl.ANY), pl.BlockSpec(memory_space=pl.ANY),
                  pl.BlockSpec((1, 1, d, d_ff), w_blk), pl.BlockSpec((1, 1, d, d_ff), w_blk),
                  pl.BlockSpec((1, 1, d_ff, d), w_blk),
                  pl.BlockSpec((1, 1, 1, d_ff), w_blk), pl.BlockSpec((1, 1, 1, d_ff), w_blk),
                  pl.BlockSpec((1, 1, 1, d), w_blk)],
        out_specs=pl.BlockSpec((bm * V7X_SUBLANES, V7X_LANES), lambda i, *_: (i, 0)),
        scratch_shapes=[pltpu.VMEM((2, bm * V7X_SUBLANES, V7X_LANES), F32), pltpu.VMEM((3, d, d_ff), BF16),
                        pltpu.SemaphoreType.DMA((2,))])
    return pl.pallas_call(
        functools.partial(_expert_kernel, bm=bm, runs_per_expert=runs_per_expert),
        grid_spec=grid_spec,
        out_shape=jax.ShapeDtypeStruct((n_blocks * bm * V7X_SUBLANES, V7X_LANES), F32),
        compiler_params=pltpu.CompilerParams(dimension_semantics=("arbitrary",),
                                             vmem_limit_bytes=V7X_VMEM_LIMIT_BYTES),
        name="experts",
    )(block_e, jlo, jhi, n_valid, n_active, gdst, glen, gsrc, xl_p, xl_s, wg, wu, wd, b3(bg), b3(bu), b3(bd))


def _combine_kernel(gdst_ref, glen_ref, off_ref, x1_ref, ys_ref, rt_ref, g2_ref, fg_ref, o_ref, ybuf, sem,
                    *, tile0, n_exp, runs_per_expert, final):
    step = pl.program_id(0)
    n_steps = pl.num_programs(0)
    n = x1_ref.shape[0]
    cur = step % 2

    def start_fetch(tile, buf_slot):
        def body(e, c):
            r = e * runs_per_expert + tile
            ln = glen_ref[r]

            @pl.when(ln > 0)
            def _():
                pltpu.make_async_copy(_rows(ys_ref, gdst_ref[r], ln), _rows(ybuf.at[buf_slot], off_ref[r], ln),
                                      sem.at[buf_slot]).start()
            return c
        lax.fori_loop(0, n_exp, body, 0)

    @pl.when(step == 0)
    def _():
        start_fetch(tile0, 0)

    @pl.when(step + 1 < n_steps)
    def _():
        start_fetch(tile0 + step + 1, 1 - cur)

    pltpu.make_async_copy(ybuf.at[cur], ybuf.at[cur], sem.at[cur]).wait()
    yl = _from_token_tiles(ybuf.at[cur], TOP_K * n)
    rt = rt_ref[...]
    col = lax.broadcasted_iota(I32, (n, TOP_K * n), 1).astype(F32)
    sel = jnp.zeros((n, TOP_K * n), F32)
    for k in range(TOP_K):
        sel = jnp.where(col == rt[:, k:k + 1], rt[:, TOP_K + k:TOP_K + k + 1], sel)
    y = jnp.dot(sel.astype(BF16), yl.astype(BF16), preferred_element_type=F32)
    x2 = x1_ref[...] + g2_ref[...] * y
    if final:
        ms = jnp.mean(x2 * x2, axis=-1, keepdims=True)
        x2 = x2 * lax.rsqrt(ms + RMS_EPS) * fg_ref[...]
    o_ref[...] = x2


def _combine(x1, ys, rt_tall, g2, final_g, tables, tile0, n_tile, rows_per_gate, final, n_exp):
    t, d = x1.shape
    gdst, glen, off = tables
    runs_per_expert = gdst.shape[0] // n_exp
    if rows_per_gate == 1:
        g2_spec = pl.BlockSpec((n_tile, d), lambda i, *_: (i, 0))
    else:
        g2 = g2.reshape(-1, 1, d)
        g2_spec = pl.BlockSpec((None, 1, d), lambda i, *_: ((i * n_tile) // rows_per_gate, 0, 0))
    grid_spec = pltpu.PrefetchScalarGridSpec(
        num_scalar_prefetch=3, grid=(t // n_tile,),
        in_specs=[pl.BlockSpec((n_tile, d), lambda i, *_: (i, 0)), pl.BlockSpec(memory_space=pl.ANY),
                  pl.BlockSpec((n_tile, 2 * TOP_K), lambda i, *_: (i, 0)), g2_spec,
                  pl.BlockSpec((1, d), lambda i, *_: (0, 0))],
        out_specs=pl.BlockSpec((n_tile, d), lambda i, *_: (i, 0)),
        scratch_shapes=[pltpu.VMEM((2, TOP_K * n_tile * V7X_SUBLANES, V7X_LANES), F32),
                        pltpu.SemaphoreType.DMA((2,))])
    return pl.pallas_call(
        functools.partial(_combine_kernel, tile0=tile0, n_exp=n_exp, runs_per_expert=runs_per_expert, final=final),
        grid_spec=grid_spec,
        out_shape=jax.ShapeDtypeStruct((t, d), F32),
        compiler_params=pltpu.CompilerParams(dimension_semantics=("arbitrary",),
                                             vmem_limit_bytes=V7X_VMEM_LIMIT_BYTES),
        name="combine",
    )(gdst, glen, off, x1, ys, rt_tall, g2, final_g.reshape(1, d))


def _routing_plan(tile_counts, tile_rows, bm, n_blocks):
    n_tiles, n_exp = tile_counts.shape
    counts = jnp.sum(tile_counts, axis=0)
    pcounts = (counts + bm - 1) // bm * bm
    pends = jnp.cumsum(pcounts)
    pstarts = pends - pcounts
    base = jnp.cumsum(tile_counts, axis=0) - tile_counts
    off = jnp.cumsum(tile_counts, axis=1) - tile_counts
    gdst = (pstarts[None, :] + base).T
    glen = tile_counts.T
    gsrc = (tile_rows[:, None] + off).T
    blk0 = jnp.arange(n_blocks, dtype=I32) * bm
    block_e = jnp.minimum(jnp.sum(pends[None, :] <= blk0[:, None], axis=1), n_exp - 1).astype(I32)
    pick = jax.nn.one_hot(block_e, n_exp, dtype=F32)
    hi = lax.Precision.HIGHEST
    gd_b = jnp.dot(pick, gdst.astype(F32), precision=hi)[:, :n_tiles - 1]
    ge_b = jnp.dot(pick, (gdst + glen).astype(F32), precision=hi)[:, :n_tiles - 1]
    blk0f = blk0.astype(F32)[:, None]
    jlo = jnp.sum(ge_b <= blk0f, axis=1).astype(I32)
    jhi = jnp.sum(gd_b < blk0f + bm, axis=1).astype(I32)
    n_valid = jnp.clip(pstarts[block_e] + counts[block_e] - blk0, 0, bm).astype(I32)
    n_active = (pends[-1] // bm).astype(I32).reshape(1)
    flat = lambda a: a.reshape(-1).astype(I32)
    return ((block_e, jlo, jhi, n_valid, n_active, flat(gdst), flat(glen), flat(gsrc)),
            (flat(gdst), flat(glen), flat(off.T)))


def kernel(x_prompt, x_sample, state_pool, state_conv, state_lru_conv, state_lru_h, c_prompt, c_sample, ada_w, ada_b, norm1_g, w_in, pool_w, pool_scale, conv_dw, conv_b, conv_ln_g, conv_ln_b, lru_conv_w, lru_conv_b, lru_wa, lru_ba, lru_wx, lru_bx, lru_lambda, w_out, norm2_g, router_w, router_b, moe_w_gate, moe_b_gate, moe_w_up, moe_b_up, moe_w_down, moe_b_down, final_g):
    p = dict(norm1_g=norm1_g, w_in=w_in, pool_w=pool_w, pool_scale=pool_scale, conv_dw=conv_dw, conv_b=conv_b,
             conv_ln_g=conv_ln_g, conv_ln_b=conv_ln_b, lru_conv_w=lru_conv_w, lru_conv_b=lru_conv_b, lru_wa=lru_wa,
             lru_ba=lru_ba, lru_wx=lru_wx, lru_bx=lru_bx, lru_lambda=lru_lambda, w_out=w_out, norm2_g=norm2_g,
             router_w=router_w, router_b=router_b)
    n_b, seq, d_model = x_prompt.shape
    n_s = x_sample.shape[0]
    depth = ada_w.shape[0]
    n_exp = router_w.shape[2]
    t_p = n_b * seq
    t_all = t_p + n_s
    ts = SEQ_TILE
    n_tiles_p = t_p // ts
    bm = MOE_BLOCK
    n_blocks = -(-(t_all * TOP_K) // bm) + n_exp
    tile_rows = jnp.concatenate([jnp.arange(n_tiles_p, dtype=I32) * (TOP_K * ts), jnp.zeros((1,), I32)])

    mod = _adaln(jnp.concatenate([c_prompt, c_sample], axis=0), ada_w, ada_b)
    xp = x_prompt.reshape(t_p, d_model)
    xs = x_sample.reshape(n_s, d_model)

    new_p, new_s = [], []
    for l in range(depth):
        weights = _layer_weights(p, l)
        mod_p = mod[l, :n_b].reshape(n_b, 6, d_model)
        mod_s = mod[l, n_b:]
        x1p, xl_p, rt_p, cnt_p, npool, nconv, nlc, nlh = _mix_prompt(xp, mod_p, weights, n_b, seq)
        x1s, xl_s, rt_s, cnt_s, spool, sconv, slc, slh = _mix_sample(
            xs, mod_s, weights, state_pool[l], state_conv[l], state_lru_conv[l], state_lru_h[l], PAST_LEN)
        new_p.append((npool, nconv, nlc, nlh.reshape(n_b, -1)))
        new_s.append((spool, sconv, slc, slh))

        tile_counts = jnp.concatenate([cnt_p[:, 0].reshape(n_tiles_p, n_exp), cnt_s[:, 0].reshape(1, n_exp)],
                                      axis=0).astype(I32)
        plan, tables = _routing_plan(tile_counts, tile_rows, bm, n_blocks)
        ys = _experts(l, xl_p, xl_s, plan, moe_w_gate, moe_w_up, moe_w_down, moe_b_gate, moe_b_up, moe_b_down,
                      n_blocks, bm)
        final = l == depth - 1
        g2_p = mod[l, :n_b, 5 * d_model:]
        g2_s = mod[l, n_b:, 5 * d_model:]
        xp = _combine(x1p, ys, rt_p.T, g2_p, final_g, tables, 0, ts, seq, final, n_exp)
        xs = _combine(x1s, ys, rt_s.T, g2_s, final_g, tables, n_tiles_p, n_s, 1, final, n_exp)

    stack = lambda items, i: jnp.stack([it[i] for it in items])
    return (xp.reshape(n_b, seq, d_model), xs.reshape(n_s, 1, d_model),
            stack(new_p, 0), stack(new_p, 1), stack(new_p, 2), stack(new_p, 3),
            stack(new_s, 0), stack(new_s, 1), stack(new_s, 2), stack(new_s, 3))
```

```python
import functools

import jax
import jax.numpy as jnp
from jax import lax
from jax.experimental import pallas as pl
from jax.experimental.pallas import tpu as pltpu

F32 = jnp.float32
BF16 = jnp.bfloat16
I32 = jnp.int32

POOL_WINDOWS = (2, 4, 8, 16)
POOL_BUF = max(POOL_WINDOWS) - 1
CONV_WIDTH = 31
CONV_BUF = CONV_WIDTH - 1
LRU_CONV_WIDTH = 4
LRU_BUF = LRU_CONV_WIDTH - 1
LRU_C = 8.0
TOP_K = 4
SWIGLU_ALPHA = 1.702
SWIGLU_LIMIT = 7.0
RMS_EPS = 1e-6
LN_EPS = 1e-5
PAST_LEN = 16384

V7X_SUBLANES = 8
V7X_LANES = 128
V7X_VMEM_LIMIT_BYTES = 56 * 1024 * 1024

SEQ_TILE = 256
MOE_BLOCK = 512
HIST_POOL = 16
HIST_CONV = 32
HIST_LRU = 8


def _bf16_round(x):
    return x.astype(BF16).astype(F32)


def _rms_mod(x, g, scale, shift):
    ms = jnp.mean(x * x, axis=-1, keepdims=True)
    return x * lax.rsqrt(ms + RMS_EPS) * g * (1.0 + scale) + shift


def _pool_window_select(accs, lane, pool_group):
    n = len(POOL_WINDOWS)
    wsum = accs[POOL_WINDOWS[-1]]
    wlen = jnp.full(lane.shape, POOL_WINDOWS[-1], I32)
    for g in reversed(range(n - 1)):
        m = lane < (g + 1) * pool_group
        wsum = jnp.where(m, accs[POOL_WINDOWS[g]], wsum)
        wlen = jnp.where(m, POOL_WINDOWS[g], wlen)
    return wsum, wlen


def _layer_norm_silu(y, g, b):
    mu = jnp.mean(y, axis=-1, keepdims=True)
    yc = y - mu
    var = jnp.mean(yc * yc, axis=-1, keepdims=True)
    return jax.nn.silu(yc * lax.rsqrt(var + LN_EPS) * g + b)


def _softplus(x):
    return jnp.maximum(x, 0.0) + jnp.log1p(jnp.exp(-jnp.abs(x)))


def _lru_gates(xc, wa_ref, ba, wx_ref, bx, lam, reset):
    half = xc.shape[1] // 2
    xb = xc.astype(BF16)

    def heads(w_ref):
        lo = jnp.dot(xb[:, :half], w_ref[0], preferred_element_type=F32)
        hi = jnp.dot(xb[:, half:], w_ref[1], preferred_element_type=F32)
        return jnp.concatenate([lo, hi], axis=1)

    r = jax.nn.sigmoid(heads(wa_ref) + ba)
    i = jax.nn.sigmoid(heads(wx_ref) + bx)
    log_a = -LRU_C * r * _softplus(-lam)
    a = jnp.exp(log_a)
    mult = jnp.sqrt(1.0 - a * a)
    if reset is not None:
        a = jnp.where(reset, 0.0, a)
        mult = jnp.where(reset, 1.0, mult)
    return a, mult * i * xc


def _scan_rows(a, b, h0):
    rows, c = a.shape
    groups = rows // V7X_SUBLANES
    a3 = a.reshape(groups, V7X_SUBLANES, c)
    b3 = b.reshape(groups, V7X_SUBLANES, c)
    sub = lax.broadcasted_iota(I32, a3.shape, 1)
    d = 1
    while d < V7X_SUBLANES:
        ar = pltpu.roll(a3, d, axis=1)
        br = pltpu.roll(b3, d, axis=1)
        m = sub >= d
        b3 = jnp.where(m, a3 * br + b3, b3)
        a3 = jnp.where(m, a3 * ar, a3)
        d *= 2
    hp = h0
    outs = []
    for g in range(groups):
        hg = a3[g] * hp + b3[g]
        outs.append(hg)
        hp = hg[V7X_SUBLANES - 1:V7X_SUBLANES, :]
    return jnp.concatenate(outs, axis=0)


def _to_token_tiles(ref, x):
    n, d = x.shape
    for s in range(d // V7X_LANES):
        ref[pl.ds(s, n, stride=V7X_SUBLANES), :] = x[:, s * V7X_LANES:(s + 1) * V7X_LANES]


def _from_token_tiles(ref, n):
    return jnp.concatenate([ref[pl.ds(s, n, stride=V7X_SUBLANES), :] for s in range(V7X_SUBLANES)], axis=1)


def _route(h2, rwt_ref, rb_ref, xl_ref, rt_ref, cnt_ref):
    n = h2.shape[0]
    n_exp = rwt_ref.shape[0]
    logits = lax.dot_general(rwt_ref[...], h2.astype(BF16), (((1,), (1,)), ((), ())),
                             preferred_element_type=F32) + rb_ref[...]
    eio = lax.broadcasted_iota(I32, (n_exp, n), 0).astype(F32)
    l = logits
    vals, sels = [], []
    for _ in range(TOP_K):
        m = jnp.max(l, axis=0, keepdims=True)
        idx = jnp.min(jnp.where(l == m, eio, float(n_exp)), axis=0, keepdims=True)
        sel = eio == idx
        vals.append(m)
        sels.append(sel)
        l = jnp.where(sel, -jnp.inf, l)
    exps = [jnp.exp(v - vals[0]) for v in vals]
    den = exps[0]
    for e in exps[1:]:
        den = den + e
    ws = [e / den for e in exps]
    onehot = jnp.zeros((n_exp, n), F32)
    for sel in sels:
        onehot = jnp.where(sel, 1.0, onehot)
    earlier = lax.broadcasted_iota(I32, (n, n), 0) < lax.broadcasted_iota(I32, (n, n), 1)
    prior = jnp.dot(onehot.astype(BF16), jnp.where(earlier, 1.0, 0.0).astype(BF16), preferred_element_type=F32)
    cnt = jnp.broadcast_to(jnp.sum(onehot, axis=1, keepdims=True), (n_exp, V7X_LANES))
    lower = lax.broadcasted_iota(I32, (n_exp, n_exp), 0) > lax.broadcasted_iota(I32, (n_exp, n_exp), 1)
    off = jnp.dot(jnp.where(lower, 1.0, 0.0), cnt, precision=lax.Precision.HIGHEST, preferred_element_type=F32)
    pos = prior + jnp.concatenate([off] * (n // V7X_LANES), axis=1)
    rhos = [jnp.sum(jnp.where(sel, pos, 0.0), axis=0, keepdims=True) for sel in sels]
    rt_ref[...] = jnp.concatenate(rhos + ws, axis=0)
    cnt_ref[...] = cnt
    rows = lax.broadcasted_iota(I32, (TOP_K * n, n), 0).astype(F32)
    perm = jnp.zeros((TOP_K * n, n), F32)
    for rho in rhos:
        perm = jnp.where(rows == rho, 1.0, perm)
    xl = jnp.dot(perm.astype(BF16), h2.astype(BF16), preferred_element_type=F32)
    _to_token_tiles(xl_ref, xl)


def _mix_prompt_kernel(x_ref, mod_ref, n1g_ref, win_ref, poolw_ref, pools_ref, cdw_ref, cb_ref, lng_ref, lnb_ref,
                       lcw_ref, lcb_ref, wa_ref, ba_ref, wx_ref, bx_ref, lam_ref, wout_ref, n2g_ref, rwt_ref,
                       rb_ref,
                       x1_ref, xl_ref, rt_ref, cnt_ref, npool_ref, nconv_ref, nlc_ref, nlh_ref,
                       e_s, a_s, xe_s, h_s, *, ts, d_pool, d_conv, d_lru):
    s = pl.program_id(1)

    @pl.when(s == 0)
    def _():
        e_s[0:HIST_POOL, :] = jnp.zeros((HIST_POOL, d_pool), F32)
        a_s[0:HIST_CONV, :] = jnp.zeros((HIST_CONV, d_conv), F32)
        xe_s[0:HIST_LRU, :] = jnp.zeros((HIST_LRU, d_lru), F32)
        h_s[...] = jnp.zeros(h_s.shape, F32)

    x = x_ref[...]
    m = mod_ref[0]
    sh1, sc1, g1, sh2, sc2, g2 = [m[i:i + 1] for i in range(6)]
    h = _rms_mod(x, n1g_ref[...], sc1, sh1)
    z = jnp.dot(h.astype(BF16), win_ref[...], preferred_element_type=F32)
    o0 = 0
    u = z[:, o0:o0 + d_pool]
    o0 += d_pool
    v = z[:, o0:o0 + d_conv]
    o0 += d_conv
    gc = z[:, o0:o0 + d_conv]
    o0 += d_conv
    xr = z[:, o0:o0 + d_lru]
    o0 += d_lru
    gl = z[:, o0:o0 + d_lru]

    e_s[HIST_POOL:HIST_POOL + ts, :] = u
    acc = u
    accs = {}
    for j in range(1, POOL_BUF + 1):
        acc = acc + e_s[pl.ds(HIST_POOL - j, ts), :]
        if j + 1 in POOL_WINDOWS:
            accs[j + 1] = acc
    lane = lax.broadcasted_iota(I32, (ts, d_pool), 1)
    pos = s * ts + lax.broadcasted_iota(I32, (ts, d_pool), 0)
    wsum, wlen = _pool_window_select(accs, lane, d_pool // len(POOL_WINDOWS))
    cnt = jnp.minimum(pos + 1, wlen).astype(F32)
    dmean = wsum / cnt - u
    o_pool = jnp.dot(dmean.astype(BF16), poolw_ref[...], preferred_element_type=F32) * pools_ref[...]
    npool_ref[0] = e_s[pl.ds(HIST_POOL + ts - POOL_BUF, POOL_BUF), :]
    e_s[0:HIST_POOL, :] = e_s[pl.ds(ts, HIST_POOL), :]

    a = v * jax.nn.sigmoid(gc)
    a_s[HIST_CONV:HIST_CONV + ts, :] = a
    y = jnp.broadcast_to(cb_ref[...], (ts, d_conv))
    for j in range(CONV_WIDTH):
        y = y + cdw_ref[j:j + 1, :] * a_s[pl.ds(HIST_CONV - CONV_BUF + j, ts), :]
    o_conv = _layer_norm_silu(y, lng_ref[...], lnb_ref[...])
    nconv_ref[0] = a[ts - CONV_BUF:ts, :]
    a_s[0:HIST_CONV, :] = a_s[pl.ds(ts, HIST_CONV), :]

    xe_s[HIST_LRU:HIST_LRU + ts, :] = _bf16_round(xr)
    xc = jnp.broadcast_to(lcb_ref[...], (ts, d_lru))
    for j in range(LRU_CONV_WIDTH):
        xc = xc + _bf16_round(lcw_ref[j:j + 1, :]) * xe_s[pl.ds(HIST_LRU - LRU_BUF + j, ts), :]
    reset = (s * ts + lax.broadcasted_iota(I32, (ts, d_lru), 0)) == 0
    a_t, b_t = _lru_gates(xc, wa_ref, ba_ref[...], wx_ref, bx_ref[...], lam_ref[...], reset)
    hseq = _scan_rows(a_t, b_t, h_s[0:1, :])
    o_lru = hseq * jax.nn.gelu(gl)
    nlc_ref[0] = xr[ts - LRU_BUF:ts, :]
    nlh_ref[0] = hseq[ts - 1:ts, :]
    h_s[...] = jnp.broadcast_to(hseq[ts - 1:ts, :], h_s.shape)
    xe_s[0:HIST_LRU, :] = xe_s[pl.ds(ts, HIST_LRU), :]

    o = jnp.concatenate([o_pool, o_conv, o_lru], axis=1).astype(BF16)
    x1 = x + g1 * jnp.dot(o, wout_ref[...], preferred_element_type=F32)
    x1_ref[...] = x1
    h2 = _rms_mod(x1, n2g_ref[...], sc2, sh2)
    _route(h2, rwt_ref, rb_ref, xl_ref, rt_ref, cnt_ref)


def _mix_sample_kernel(x_ref, mod_ref, n1g_ref, win_ref, poolw_ref, pools_ref, cdw_ref, cb_ref, lng_ref, lnb_ref,
                       lcw_ref, lcb_ref, wa_ref, ba_ref, wx_ref, bx_ref, lam_ref, wout_ref, n2g_ref, rwt_ref,
                       rb_ref, sp_ref, sc_ref, sl_ref, sh_ref,
                       x1_ref, xl_ref, rt_ref, cnt_ref, npool_ref, nconv_ref, nlc_ref, nlh_ref,
                       *, start, d_model, d_pool, d_conv, d_lru):
    n = x_ref.shape[0]
    x = x_ref[...]
    m = mod_ref[...]
    sh1, sc1, g1, sh2, sc2, g2 = [m[:, i * d_model:(i + 1) * d_model] for i in range(6)]
    h = _rms_mod(x, n1g_ref[...], sc1, sh1)
    z = jnp.dot(h.astype(BF16), win_ref[...], preferred_element_type=F32)
    o0 = 0
    u = z[:, o0:o0 + d_pool]
    o0 += d_pool
    v = z[:, o0:o0 + d_conv]
    o0 += d_conv
    gc = z[:, o0:o0 + d_conv]
    o0 += d_conv
    xr = z[:, o0:o0 + d_lru]
    o0 += d_lru
    gl = z[:, o0:o0 + d_lru]

    acc = u
    accs = {}
    for j in range(1, POOL_BUF + 1):
        acc = acc + sp_ref[:, POOL_BUF - j, :]
        if j + 1 in POOL_WINDOWS:
            accs[j + 1] = acc
    lane = lax.broadcasted_iota(I32, (n, d_pool), 1)
    wsum, wlen = _pool_window_select(accs, lane, d_pool // len(POOL_WINDOWS))
    cnt = jnp.minimum(start + 1, wlen).astype(F32)
    dmean = wsum / cnt - u
    o_pool = jnp.dot(dmean.astype(BF16), poolw_ref[...], preferred_element_type=F32) * pools_ref[...]
    for j in range(POOL_BUF - 1):
        npool_ref[:, j, :] = sp_ref[:, j + 1, :]
    npool_ref[:, POOL_BUF - 1, :] = u

    a = v * jax.nn.sigmoid(gc)
    y = cb_ref[...] + cdw_ref[CONV_BUF:CONV_WIDTH, :] * a
    for j in range(CONV_BUF):
        y = y + cdw_ref[j:j + 1, :] * sc_ref[:, j, :]
    o_conv = _layer_norm_silu(y, lng_ref[...], lnb_ref[...])
    for j in range(CONV_BUF - 1):
        nconv_ref[:, j, :] = sc_ref[:, j + 1, :]
    nconv_ref[:, CONV_BUF - 1, :] = a

    xc = lcb_ref[...] + lcw_ref[LRU_BUF:LRU_CONV_WIDTH, :] * xr
    for j in range(LRU_BUF):
        xc = xc + lcw_ref[j:j + 1, :] * sl_ref[:, j, :]
    reset = jnp.full((n, d_lru), True) if start == 0 else None
    a_t, b_t = _lru_gates(xc, wa_ref, ba_ref[...], wx_ref, bx_ref[...], lam_ref[...], reset)
    hnew = a_t * sh_ref[...] + b_t
    o_lru = hnew * jax.nn.gelu(gl)
    for j in range(LRU_BUF - 1):
        nlc_ref[:, j, :] = sl_ref[:, j + 1, :]
    nlc_ref[:, LRU_BUF - 1, :] = xr
    nlh_ref[...] = hnew

    o = jnp.concatenate([o_pool, o_conv, o_lru], axis=1).astype(BF16)
    x1 = x + g1 * jnp.dot(o, wout_ref[...], preferred_element_type=F32)
    x1_ref[...] = x1
    h2 = _rms_mod(x1, n2g_ref[...], sc2, sh2)
    _route(h2, rwt_ref, rb_ref, xl_ref, rt_ref, cnt_ref)


def _adaln_kernel(c_ref, w_ref, b_ref, o_ref):
    c = c_ref[...]
    o_ref[0] = jnp.dot(jax.nn.silu(c).astype(BF16), w_ref[0].astype(BF16), preferred_element_type=F32) + b_ref[0]


def _adaln(c_all, ada_w, ada_b):
    depth, d, n6 = ada_w.shape
    nb = c_all.shape[0]
    tn = d
    return pl.pallas_call(
        _adaln_kernel,
        grid=(depth, n6 // tn),
        in_specs=[pl.BlockSpec((nb, d), lambda l, j: (0, 0)),
                  pl.BlockSpec((1, d, tn), lambda l, j: (l, 0, j)),
                  pl.BlockSpec((1, 1, tn), lambda l, j: (l, 0, j))],
        out_specs=pl.BlockSpec((1, nb, tn), lambda l, j: (l, 0, j)),
        out_shape=jax.ShapeDtypeStruct((depth, nb, n6), F32),
        compiler_params=pltpu.CompilerParams(dimension_semantics=("arbitrary", "arbitrary")),
        name="adaln",
    )(c_all, ada_w, ada_b.reshape(depth, 1, n6))


def _const_spec(shape):
    nd = len(shape)
    return pl.BlockSpec(shape, lambda *_: (0,) * nd)


def _block_diag(w):
    g, a, b = w.shape
    return jnp.einsum("gij,gh->gihj", w, jnp.eye(g, dtype=w.dtype)).reshape(g * a, g * b)


def _layer_weights(p, l):
    def head_tiles(w):
        nh = w.shape[0]
        return jnp.stack([_block_diag(w[:nh // 2]), _block_diag(w[nh // 2:])]).astype(BF16)

    row = lambda a: a[l].reshape(1, -1)
    return [row(p["norm1_g"]), p["w_in"][l].astype(BF16), _block_diag(p["pool_w"][l]).astype(BF16),
            row(p["pool_scale"]),
            p["conv_dw"][l], row(p["conv_b"]), row(p["conv_ln_g"]), row(p["conv_ln_b"]),
            p["lru_conv_w"][l], row(p["lru_conv_b"]), head_tiles(p["lru_wa"][l]), row(p["lru_ba"]),
            head_tiles(p["lru_wx"][l]), row(p["lru_bx"]), row(p["lru_lambda"]),
            p["w_out"][l].astype(BF16), row(p["norm2_g"]), p["router_w"][l].T.astype(BF16),
            p["router_b"][l].reshape(-1, 1)]


def _mix_prompt(x, mod, weights, n_batch, seq):
    t, d_model = x.shape
    ts = SEQ_TILE
    n_s = seq // ts
    d_pool = weights[3].shape[1]
    d_conv = weights[5].shape[1]
    d_lru = weights[9].shape[1]
    n_exp = weights[17].shape[0]
    tile_rows = TOP_K * ts * V7X_SUBLANES
    tok = lambda b, s: (b * n_s + s, 0)
    lanes = lambda b, s: (0, b * n_s + s)
    per_b = lambda b, s: (b, 0, 0)
    in_specs = [pl.BlockSpec((ts, d_model), tok), pl.BlockSpec((1, 6, d_model), per_b)]
    in_specs += [_const_spec(w.shape) for w in weights]
    out_shape = (jax.ShapeDtypeStruct((t, d_model), F32),
                 jax.ShapeDtypeStruct((t * TOP_K * V7X_SUBLANES, V7X_LANES), F32),
                 jax.ShapeDtypeStruct((2 * TOP_K, t), F32),
                 jax.ShapeDtypeStruct((t // ts * n_exp, V7X_LANES), F32),
                 jax.ShapeDtypeStruct((n_batch, POOL_BUF, d_pool), F32),
                 jax.ShapeDtypeStruct((n_batch, CONV_BUF, d_conv), F32),
                 jax.ShapeDtypeStruct((n_batch, LRU_BUF, d_lru), F32),
                 jax.ShapeDtypeStruct((n_batch, 1, d_lru), F32))
    out_specs = (pl.BlockSpec((ts, d_model), tok), pl.BlockSpec((tile_rows, V7X_LANES), tok),
                 pl.BlockSpec((2 * TOP_K, ts), lanes), pl.BlockSpec((n_exp, V7X_LANES), tok),
                 pl.BlockSpec((1, POOL_BUF, d_pool), per_b), pl.BlockSpec((1, CONV_BUF, d_conv), per_b),
                 pl.BlockSpec((1, LRU_BUF, d_lru), per_b), pl.BlockSpec((1, 1, d_lru), per_b))
    scratch = [pltpu.VMEM((HIST_POOL + ts, d_pool), F32), pltpu.VMEM((HIST_CONV + ts, d_conv), F32),
               pltpu.VMEM((HIST_LRU + ts, d_lru), F32), pltpu.VMEM((V7X_SUBLANES, d_lru), F32)]
    return pl.pallas_call(
        functools.partial(_mix_prompt_kernel, ts=ts, d_pool=d_pool, d_conv=d_conv, d_lru=d_lru),
        grid=(n_batch, n_s), in_specs=in_specs, out_specs=out_specs, out_shape=out_shape, scratch_shapes=scratch,
        compiler_params=pltpu.CompilerParams(dimension_semantics=("arbitrary", "arbitrary"),
                                             vmem_limit_bytes=V7X_VMEM_LIMIT_BYTES),
        name="mix_prompt",
    )(x, mod, *weights)


def _mix_sample(x, mod, weights, st_pool, st_conv, st_lruc, st_lruh, start):
    n, d_model = x.shape
    d_pool = weights[3].shape[1]
    d_conv = weights[5].shape[1]
    d_lru = weights[9].shape[1]
    n_exp = weights[17].shape[0]
    ins = [x, mod] + list(weights) + [st_pool, st_conv, st_lruc, st_lruh]
    in_specs = [_const_spec(a.shape) for a in ins]
    shapes = [(n, d_model), (n * TOP_K * V7X_SUBLANES, V7X_LANES), (2 * TOP_K, n), (n_exp, V7X_LANES),
              st_pool.shape, st_conv.shape, st_lruc.shape, st_lruh.shape]
    return pl.pallas_call(
        functools.partial(_mix_sample_kernel, start=start, d_model=d_model, d_pool=d_pool, d_conv=d_conv,
                          d_lru=d_lru),
        grid=(1,), in_specs=in_specs, out_specs=tuple(_const_spec(s) for s in shapes),
        out_shape=tuple(jax.ShapeDtypeStruct(s, F32) for s in shapes),
        compiler_params=pltpu.CompilerParams(dimension_semantics=("arbitrary",),
                                             vmem_limit_bytes=V7X_VMEM_LIMIT_BYTES),
        name="mix_sample",
    )(*ins)


def _rows(ref, row0, n_rows):
    start = pl.multiple_of(row0 * V7X_SUBLANES, V7X_SUBLANES)
    size = pl.multiple_of(n_rows * V7X_SUBLANES, V7X_SUBLANES)
    return ref.at[pl.ds(start, size)]


def _expert_kernel(be_ref, jlo_ref, jhi_ref, nv_ref, na_ref, gdst_ref, glen_ref, gsrc_ref,
                   xlp_ref, xls_ref, wg_ref, wu_ref, wd_ref, bg_ref, bu_ref, bd_ref, ys_ref,
                   xbuf, wbf, gsem, *, bm, runs_per_expert):
    i = pl.program_id(0)
    n_active = na_ref[0]
    cur = i % 2

    def start_gather(blk, buf_slot):
        blk0 = blk * bm
        run0 = be_ref[blk] * runs_per_expert

        def fetch(r, src_ref):
            g0 = gdst_ref[r]
            lo = jnp.maximum(g0, blk0)
            n = jnp.minimum(g0 + glen_ref[r], blk0 + bm) - lo

            @pl.when(n > 0)
            def _():
                pltpu.make_async_copy(_rows(src_ref, gsrc_ref[r] + lo - g0, n),
                                      _rows(xbuf.at[buf_slot], lo - blk0, n), gsem.at[buf_slot]).start()

        def body(j, c):
            fetch(run0 + j, xlp_ref)
            return c
        lax.fori_loop(jlo_ref[blk], jhi_ref[blk], body, 0)
        fetch(run0 + runs_per_expert - 1, xls_ref)

    def wait_gather(blk, buf_slot):
        @pl.when(nv_ref[blk] > 0)
        def _():
            rows = _rows(xbuf.at[buf_slot], 0, nv_ref[blk])
            pltpu.make_async_copy(rows, rows, gsem.at[buf_slot]).wait()

    @pl.when(i == 0)
    def _():
        xbuf[...] = jnp.zeros(xbuf.shape, F32)
        start_gather(0, 0)

    @pl.when(i + 1 < n_active)
    def _():
        start_gather(i + 1, 1 - cur)

    @pl.when(i < n_active)
    def _():
        changed = jnp.logical_or(i == 0, be_ref[i] != be_ref[jnp.maximum(i - 1, 0)])

        @pl.when(changed)
        def _():
            wbf[0] = wg_ref[0, 0].astype(BF16)
            wbf[1] = wu_ref[0, 0].astype(BF16)
            wbf[2] = wd_ref[0, 0].astype(BF16)

        wait_gather(i, cur)
        x = _from_token_tiles(xbuf.at[cur], bm).astype(BF16)
        g = jnp.dot(x, wbf[0], preferred_element_type=F32) + bg_ref[0, 0]
        u = jnp.dot(x, wbf[1], preferred_element_type=F32) + bu_ref[0, 0]
        g = jnp.minimum(g, SWIGLU_LIMIT)
        u = jnp.clip(u, -SWIGLU_LIMIT, SWIGLU_LIMIT)
        act = (u + 1.0) * (g * jax.nn.sigmoid(SWIGLU_ALPHA * g))
        y = jnp.dot(act.astype(BF16), wbf[2], preferred_element_type=F32) + bd_ref[0, 0]
        _to_token_tiles(ys_ref, y)

    @pl.when(i >= n_active)
    def _():
        ys_ref[...] = jnp.zeros(ys_ref.shape, F32)


def _experts(layer, xl_p, xl_s, plan, wg, wu, wd, bg, bu, bd, n_blocks, bm):
    _, n_exp, d, d_ff = wg.shape
    block_e, jlo, jhi, n_valid, n_active, gdst, glen, gsrc = plan
    runs_per_expert = gdst.shape[0] // n_exp
    w_blk = lambda i, be, *_: (layer, be[i], 0, 0)
    b3 = lambda b: b.reshape(b.shape[0], n_exp, 1, b.shape[-1])
    grid_spec = pltpu.PrefetchScalarGridSpec(
        num_scalar_prefetch=8, grid=(n_blocks,),
        in_specs=[pl.BlockSpec(memory_space=pl.ANY), pl.BlockSpec(memory_space=pl.ANY),
                  pl.BlockSpec((1, 1, d, d_ff), w_blk), pl.BlockSpec((1, 1, d, d_ff), w_blk),
                  pl.BlockSpec((1, 1, d_ff, d), w_blk),
                  pl.BlockSpec((1, 1, 1, d_ff), w_blk), pl.BlockSpec((1, 1, 1, d_ff), w_blk),
                  pl.BlockSpec((1, 1, 1, d), w_blk)],
        out_specs=pl.BlockSpec((bm * V7X_SUBLANES, V7X_LANES), lambda i, *_: (i, 0)),
        scratch_shapes=[pltpu.VMEM((2, bm * V7X_SUBLANES, V7X_LANES), F32), pltpu.VMEM((3, d, d_ff), BF16),
                        pltpu.SemaphoreType.DMA((2,))])
    return pl.pallas_call(
        functools.partial(_expert_kernel, bm=bm, runs_per_expert=runs_per_expert),
        grid_spec=grid_spec,
        out_shape=jax.ShapeDtypeStruct((n_blocks * bm * V7X_SUBLANES, V7X_LANES), F32),
        compiler_params=pltpu.CompilerParams(dimension_semantics=("arbitrary",),
                                             vmem_limit_bytes=V7X_VMEM_LIMIT_BYTES),
        name="experts",
    )(block_e, jlo, jhi, n_valid, n_active, gdst, glen, gsrc, xl_p, xl_s, wg, wu, wd, b3(bg), b3(bu), b3(bd))


def _combine_kernel(gdst_ref, glen_ref, off_ref, x1_ref, ys_ref, rt_ref, g2_ref, fg_ref, o_ref, ybuf, sem,
                    *, tile0, n_exp, runs_per_expert, final):
    step = pl.program_id(0)
    n_steps = pl.num_programs(0)
    n = x1_ref.shape[0]
    cur = step % 2

    def start_fetch(tile, buf_slot):
        def body(e, c):
            r = e * runs_per_expert + tile
            ln = glen_ref[r]

            @pl.when(ln > 0)
            def _():
                pltpu.make_async_copy(_rows(ys_ref, gdst_ref[r], ln), _rows(ybuf.at[buf_slot], off_ref[r], ln),
                                      sem.at[buf_slot]).start()
            return c
        lax.fori_loop(0, n_exp, body, 0)

    @pl.when(step == 0)
    def _():
        start_fetch(tile0, 0)

    @pl.when(step + 1 < n_steps)
    def _():
        start_fetch(tile0 + step + 1, 1 - cur)

    pltpu.make_async_copy(ybuf.at[cur], ybuf.at[cur], sem.at[cur]).wait()
    yl = _from_token_tiles(ybuf.at[cur], TOP_K * n)
    rt = rt_ref[...]
    col = lax.broadcasted_iota(I32, (n, TOP_K * n), 1).astype(F32)
    sel = jnp.zeros((n, TOP_K * n), F32)
    for k in range(TOP_K):
        sel = jnp.where(col == rt[:, k:k + 1], rt[:, TOP_K + k:TOP_K + k + 1], sel)
    y = jnp.dot(sel.astype(BF16), yl.astype(BF16), preferred_element_type=F32)
    x2 = x1_ref[...] + g2_ref[...] * y
    if final:
        ms = jnp.mean(x2 * x2, axis=-1, keepdims=True)
        x2 = x2 * lax.rsqrt(ms + RMS_EPS) * fg_ref[...]
    o_ref[...] = x2


def _combine(x1, ys, rt_tall, g2, final_g, tables, tile0, n_tile, rows_per_gate, final, n_exp):
    t, d = x1.shape
    gdst, glen, off = tables
    runs_per_expert = gdst.shape[0] // n_exp
    if rows_per_gate == 1:
        g2_spec = pl.BlockSpec((n_tile, d), lambda i, *_: (i, 0))
    else:
        g2 = g2.reshape(-1, 1, d)
        g2_spec = pl.BlockSpec((None, 1, d), lambda i, *_: ((i * n_tile) // rows_per_gate, 0, 0))
    grid_spec = pltpu.PrefetchScalarGridSpec(
        num_scalar_prefetch=3, grid=(t // n_tile,),
        in_specs=[pl.BlockSpec((n_tile, d), lambda i, *_: (i, 0)), pl.BlockSpec(memory_space=pl.ANY),
                  pl.BlockSpec((n_tile, 2 * TOP_K), lambda i, *_: (i, 0)), g2_spec,
                  pl.BlockSpec((1, d), lambda i, *_: (0, 0))],
        out_specs=pl.BlockSpec((n_tile, d), lambda i, *_: (i, 0)),
        scratch_shapes=[pltpu.VMEM((2, TOP_K * n_tile * V7X_SUBLANES, V7X_LANES), F32),
                        pltpu.SemaphoreType.DMA((2,))])
    return pl.pallas_call(
        functools.partial(_combine_kernel, tile0=tile0, n_exp=n_exp, runs_per_expert=runs_per_expert, final=final),
        grid_spec=grid_spec,
        out_shape=jax.ShapeDtypeStruct((t, d), F32),
        compiler_params=pltpu.CompilerParams(dimension_semantics=("arbitrary",),
                                             vmem_limit_bytes=V7X_VMEM_LIMIT_BYTES),
        name="combine",
    )(gdst, glen, off, x1, ys, rt_tall, g2, final_g.reshape(1, d))


def _routing_plan(tile_counts, tile_rows, bm, n_blocks):
    n_tiles, n_exp = tile_counts.shape
    counts = jnp.sum(tile_counts, axis=0)
    pcounts = (counts + bm - 1) // bm * bm
    pends = jnp.cumsum(pcounts)
    pstarts = pends - pcounts
    base = jnp.cumsum(tile_counts, axis=0) - tile_counts
    off = jnp.cumsum(tile_counts, axis=1) - tile_counts
    gdst = (pstarts[None, :] + base).T
    glen = tile_counts.T
    gsrc = (tile_rows[:, None] + off).T
    blk0 = jnp.arange(n_blocks, dtype=I32) * bm
    block_e = jnp.minimum(jnp.sum(pends[None, :] <= blk0[:, None], axis=1), n_exp - 1).astype(I32)
    pick = jax.nn.one_hot(block_e, n_exp, dtype=F32)
    hi = lax.Precision.HIGHEST
    gd_b = jnp.dot(pick, gdst.astype(F32), precision=hi)[:, :n_tiles - 1]
    ge_b = jnp.dot(pick, (gdst + glen).astype(F32), precision=hi)[:, :n_tiles - 1]
    blk0f = blk0.astype(F32)[:, None]
    jlo = jnp.sum(ge_b <= blk0f, axis=1).astype(I32)
    jhi = jnp.sum(gd_b < blk0f + bm, axis=1).astype(I32)
    n_valid = jnp.clip(pstarts[block_e] + counts[block_e] - blk0, 0, bm).astype(I32)
    n_active = (pends[-1] // bm).astype(I32).reshape(1)
    flat = lambda a: a.reshape(-1).astype(I32)
    return ((block_e, jlo, jhi, n_valid, n_active, flat(gdst), flat(glen), flat(gsrc)),
            (flat(gdst), flat(glen), flat(off.T)))


def kernel(x_prompt, x_sample, state_pool, state_conv, state_lru_conv, state_lru_h, c_prompt, c_sample, ada_w, ada_b, norm1_g, w_in, pool_w, pool_scale, conv_dw, conv_b, conv_ln_g, conv_ln_b, lru_conv_w, lru_conv_b, lru_wa, lru_ba, lru_wx, lru_bx, lru_lambda, w_out, norm2_g, router_w, router_b, moe_w_gate, moe_b_gate, moe_w_up, moe_b_up, moe_w_down, moe_b_down, final_g):
    p = dict(norm1_g=norm1_g, w_in=w_in, pool_w=pool_w, pool_scale=pool_scale, conv_dw=conv_dw, conv_b=conv_b,
             conv_ln_g=conv_ln_g, conv_ln_b=conv_ln_b, lru_conv_w=lru_conv_w, lru_conv_b=lru_conv_b, lru_wa=lru_wa,
             lru_ba=lru_ba, lru_wx=lru_wx, lru_bx=lru_bx, lru_lambda=lru_lambda, w_out=w_out, norm2_g=norm2_g,
             router_w=router_w, router_b=router_b)
    n_b, seq, d_model = x_prompt.shape
    n_s = x_sample.shape[0]
    depth = ada_w.shape[0]
    n_exp = router_w.shape[2]
    t_p = n_b * seq
    t_all = t_p + n_s
    ts = SEQ_TILE
    n_tiles_p = t_p // ts
    bm = MOE_BLOCK
    n_blocks = -(-(t_all * TOP_K) // bm) + n_exp
    tile_rows = jnp.concatenate([jnp.arange(n_tiles_p, dtype=I32) * (TOP_K * ts), jnp.zeros((1,), I32)])

    mod = _adaln(jnp.concatenate([c_prompt, c_sample], axis=0), ada_w, ada_b)
    xp = x_prompt.reshape(t_p, d_model)
    xs = x_sample.reshape(n_s, d_model)

    new_p, new_s = [], []
    for l in range(depth):
        weights = _layer_weights(p, l)
        mod_p = mod[l, :n_b].reshape(n_b, 6, d_model)
        mod_s = mod[l, n_b:]
        x1p, xl_p, rt_p, cnt_p, npool, nconv, nlc, nlh = _mix_prompt(xp, mod_p, weights, n_b, seq)
        x1s, xl_s, rt_s, cnt_s, spool, sconv, slc, slh = _mix_sample(
            xs, mod_s, weights, state_pool[l], state_conv[l], state_lru_conv[l], state_lru_h[l], PAST_LEN)
        new_p.append((npool, nconv, nlc, nlh.reshape(n_b, -1)))
        new_s.append((spool, sconv, slc, slh))

        tile_counts = jnp.concatenate([cnt_p[:, 0].reshape(n_tiles_p, n_exp), cnt_s[:, 0].reshape(1, n_exp)],
                                      axis=0).astype(I32)
        plan, tables = _routing_plan(tile_counts, tile_rows, bm, n_blocks)
        ys = _experts(l, xl_p, xl_s, plan, moe_w_gate, moe_w_up, moe_w_down, moe_b_gate, moe_b_up, moe_b_down,
                      n_blocks, bm)
        final = l == depth - 1
        g2_p = mod[l, :n_b, 5 * d_model:]
        g2_s = mod[l, n_b:, 5 * d_model:]
        xp = _combine(x1p, ys, rt_p.T, g2_p, final_g, tables, 0, ts, seq, final, n_exp)
        xs = _combine(x1s, ys, rt_s.T, g2_s, final_g, tables, n_tiles_p, n_s, 1, final, n_exp)

    stack = lambda items, i: jnp.stack([it[i] for it in items])
    return (xp.reshape(n_b, seq, d_model), xs.reshape(n_s, 1, d_model),
            stack(new_p, 0), stack(new_p, 1), stack(new_p, 2), stack(new_p, 3),
            stack(new_s, 0), stack(new_s, 1), stack(new_s, 2), stack(new_s, 3))
```

```python
import functools

import jax
import jax.numpy as jnp
from jax import lax
from jax.experimental import pallas as pl
from jax.experimental.pallas import tpu as pltpu

F32 = jnp.float32
BF16 = jnp.bfloat16
I32 = jnp.int32

POOL_WINDOWS = (2, 4, 8, 16)
POOL_BUF = max(POOL_WINDOWS) - 1
CONV_WIDTH = 31
CONV_BUF = CONV_WIDTH - 1
LRU_CONV_WIDTH = 4
LRU_BUF = LRU_CONV_WIDTH - 1
LRU_C = 8.0
TOP_K = 4
SWIGLU_ALPHA = 1.702
SWIGLU_LIMIT = 7.0
RMS_EPS = 1e-6
LN_EPS = 1e-5
PAST_LEN = 16384

V7X_SUBLANES = 8
V7X_LANES = 128
V7X_VMEM_LIMIT_BYTES = 56 * 1024 * 1024

SEQ_TILE = 256
MOE_BLOCK = 512
HIST_POOL = 16
HIST_CONV = 32
HIST_LRU = 8


def _bf16_round(x):
    return x.astype(BF16).astype(F32)


def _rms_mod(x, g, scale, shift):
    ms = jnp.mean(x * x, axis=-1, keepdims=True)
    return x * lax.rsqrt(ms + RMS_EPS) * g * (1.0 + scale) + shift


def _pool_window_select(accs, lane, pool_group):
    n = len(POOL_WINDOWS)
    wsum = accs[POOL_WINDOWS[-1]]
    wlen = jnp.full(lane.shape, POOL_WINDOWS[-1], I32)
    for g in reversed(range(n - 1)):
        m = lane < (g + 1) * pool_group
        wsum = jnp.where(m, accs[POOL_WINDOWS[g]], wsum)
        wlen = jnp.where(m, POOL_WINDOWS[g], wlen)
    return wsum, wlen


def _layer_norm_silu(y, g, b):
    mu = jnp.mean(y, axis=-1, keepdims=True)
    yc = y - mu
    var = jnp.mean(yc * yc, axis=-1, keepdims=True)
    return jax.nn.silu(yc * lax.rsqrt(var + LN_EPS) * g + b)


def _softplus(x):
    return jnp.maximum(x, 0.0) + jnp.log1p(jnp.exp(-jnp.abs(x)))


def _lru_gates(xc, wa_ref, ba, wx_ref, bx, lam, reset):
    half = xc.shape[1] // 2
    xb = xc.astype(BF16)

    def heads(w_ref):
        lo = jnp.dot(xb[:, :half], w_ref[0], preferred_element_type=F32)
        hi = jnp.dot(xb[:, half:], w_ref[1], preferred_element_type=F32)
        return jnp.concatenate([lo, hi], axis=1)

    r = jax.nn.sigmoid(heads(wa_ref) + ba)
    i = jax.nn.sigmoid(heads(wx_ref) + bx)
    log_a = -LRU_C * r * _softplus(-lam)
    a = jnp.exp(log_a)
    t = 1.0 - a * a
    mult = jnp.where(t > 0.0, t * lax.rsqrt(t), 0.0)
    if reset is not None:
        a = jnp.where(reset, 0.0, a)
        mult = jnp.where(reset, 1.0, mult)
    return a, mult * i * xc


def _scan_rows(a, b, h0):
    rows, c = a.shape
    groups = rows // V7X_SUBLANES
    a3 = a.reshape(groups, V7X_SUBLANES, c)
    b3 = b.reshape(groups, V7X_SUBLANES, c)
    sub = lax.broadcasted_iota(I32, a3.shape, 1)
    d = 1
    while d < V7X_SUBLANES:
        ar = pltpu.roll(a3, d, axis=1)
        br = pltpu.roll(b3, d, axis=1)
        m = sub >= d
        b3 = jnp.where(m, a3 * br + b3, b3)
        a3 = jnp.where(m, a3 * ar, a3)
        d *= 2
    hp = h0
    outs = []
    for g in range(groups):
        hg = a3[g] * hp + b3[g]
        outs.append(hg)
        hp = hg[V7X_SUBLANES - 1:V7X_SUBLANES, :]
    return jnp.concatenate(outs, axis=0)


def _causal_taps(ext_ref, w_ref, bias, first_row, n_out, tmp_ref):
    n_taps = w_ref.shape[0]
    y = jnp.broadcast_to(bias, (n_out, ext_ref.shape[1]))
    for r in range(V7X_SUBLANES):
        rows = n_out if r == 0 else n_out + V7X_SUBLANES
        part = None
        for j in range(n_taps):
            if (first_row + j) % V7X_SUBLANES == r:
                term = w_ref[j:j + 1, :] * ext_ref[pl.ds(first_row + j - r, rows), :]
                part = term if part is None else part + term
        if part is None:
            continue
        if r == 0:
            y = y + part
        else:
            tmp_ref[r - 1] = part
            y = y + tmp_ref[r - 1, pl.ds(r, n_out), :]
    return y


def _to_token_tiles(ref, x):
    n, d = x.shape
    for s in range(d // V7X_LANES):
        ref[pl.ds(s, n, stride=V7X_SUBLANES), :] = x[:, s * V7X_LANES:(s + 1) * V7X_LANES]


def _from_token_tiles(ref, n):
    return jnp.concatenate([ref[pl.ds(s, n, stride=V7X_SUBLANES), :] for s in range(V7X_SUBLANES)], axis=1)


def _route(h2, rwt_ref, rb_ref, xl_ref, rt_ref, cnt_ref):
    n = h2.shape[0]
    n_exp = rwt_ref.shape[0]
    logits = lax.dot_general(rwt_ref[...], h2.astype(BF16), (((1,), (1,)), ((), ())),
                             preferred_element_type=F32) + rb_ref[...]
    eio = lax.broadcasted_iota(I32, (n_exp, n), 0).astype(F32)
    l = logits
    vals, sels = [], []
    for _ in range(TOP_K):
        m = jnp.max(l, axis=0, keepdims=True)
        idx = jnp.min(jnp.where(l == m, eio, float(n_exp)), axis=0, keepdims=True)
        sel = eio == idx
        vals.append(m)
        sels.append(sel)
        l = jnp.where(sel, -jnp.inf, l)
    exps = [jnp.exp(v - vals[0]) for v in vals]
    den = exps[0]
    for e in exps[1:]:
        den = den + e
    ws = [e / den for e in exps]
    onehot = jnp.zeros((n_exp, n), F32)
    for sel in sels:
        onehot = jnp.where(sel, 1.0, onehot)
    earlier = lax.broadcasted_iota(I32, (n, n), 0) < lax.broadcasted_iota(I32, (n, n), 1)
    prior = jnp.dot(onehot.astype(BF16), jnp.where(earlier, 1.0, 0.0).astype(BF16), preferred_element_type=F32)
    cnt = jnp.broadcast_to(jnp.sum(onehot, axis=1, keepdims=True), (n_exp, V7X_LANES))
    lower = lax.broadcasted_iota(I32, (n_exp, n_exp), 0) > lax.broadcasted_iota(I32, (n_exp, n_exp), 1)
    off = jnp.dot(jnp.where(lower, 1.0, 0.0), cnt, precision=lax.Precision.HIGHEST, preferred_element_type=F32)
    pos = prior + jnp.concatenate([off] * (n // V7X_LANES), axis=1)
    rhos = [jnp.sum(jnp.where(sel, pos, 0.0), axis=0, keepdims=True) for sel in sels]
    rt_ref[...] = jnp.concatenate(rhos + ws, axis=0)
    cnt_ref[...] = cnt
    rows = lax.broadcasted_iota(I32, (TOP_K * n, n), 0).astype(F32)
    perm = jnp.zeros((TOP_K * n, n), F32)
    for rho in rhos:
        perm = jnp.where(rows == rho, 1.0, perm)
    xl = jnp.dot(perm.astype(BF16), h2.astype(BF16), preferred_element_type=F32)
    _to_token_tiles(xl_ref, xl)


def _mix_prompt_kernel(x_ref, mod_ref, n1g_ref, win_ref, poolw_ref, pools_ref, cdw_ref, cb_ref, lng_ref, lnb_ref,
                       lcw_ref, lcb_ref, wa_ref, ba_ref, wx_ref, bx_ref, lam_ref, wout_ref, n2g_ref, rwt_ref,
                       rb_ref,
                       x1_ref, xl_ref, rt_ref, cnt_ref, npool_ref, nconv_ref, nlc_ref, nlh_ref,
                       e_s, a_s, xe_s, h_s, t_s, *, ts, d_pool, d_conv, d_lru):
    s = pl.program_id(1)

    @pl.when(s == 0)
    def _():
        e_s[0:HIST_POOL, :] = jnp.zeros((HIST_POOL, d_pool), F32)
        a_s[0:HIST_CONV, :] = jnp.zeros((HIST_CONV, d_conv), F32)
        xe_s[0:HIST_LRU, :] = jnp.zeros((HIST_LRU, d_lru), F32)
        h_s[...] = jnp.zeros(h_s.shape, F32)

    x = x_ref[...]
    m = mod_ref[0]
    sh1, sc1, g1, sh2, sc2, g2 = [m[i:i + 1] for i in range(6)]
    h = _rms_mod(x, n1g_ref[...], sc1, sh1)
    z = jnp.dot(h.astype(BF16), win_ref[...], preferred_element_type=F32)
    o0 = 0
    u = z[:, o0:o0 + d_pool]
    o0 += d_pool
    v = z[:, o0:o0 + d_conv]
    o0 += d_conv
    gc = z[:, o0:o0 + d_conv]
    o0 += d_conv
    xr = z[:, o0:o0 + d_lru]
    o0 += d_lru
    gl = z[:, o0:o0 + d_lru]

    e_s[HIST_POOL:HIST_POOL + ts, :] = u
    acc = u
    accs = {}
    for j in range(1, POOL_BUF + 1):
        acc = acc + e_s[pl.ds(HIST_POOL - j, ts), :]
        if j + 1 in POOL_WINDOWS:
            accs[j + 1] = acc
    lane = lax.broadcasted_iota(I32, (ts, d_pool), 1)
    pos = s * ts + lax.broadcasted_iota(I32, (ts, d_pool), 0)
    wsum, wlen = _pool_window_select(accs, lane, d_pool // len(POOL_WINDOWS))
    cnt = jnp.minimum(pos + 1, wlen).astype(F32)
    dmean = wsum / cnt - u
    o_pool = jnp.dot(dmean.astype(BF16), poolw_ref[...], preferred_element_type=F32) * pools_ref[...]
    npool_ref[0] = e_s[pl.ds(HIST_POOL + ts - POOL_BUF, POOL_BUF), :]
    e_s[0:HIST_POOL, :] = e_s[pl.ds(ts, HIST_POOL), :]

    a = v * jax.nn.sigmoid(gc)
    a_s[HIST_CONV:HIST_CONV + ts, :] = a
    y = _causal_taps(a_s, cdw_ref, cb_ref[...], HIST_CONV - CONV_BUF, ts, t_s)
    o_conv = _layer_norm_silu(y, lng_ref[...], lnb_ref[...])
    nconv_ref[0] = a[ts - CONV_BUF:ts, :]
    a_s[0:HIST_CONV, :] = a_s[pl.ds(ts, HIST_CONV), :]

    xe_s[HIST_LRU:HIST_LRU + ts, :] = _bf16_round(xr)
    xc = jnp.broadcast_to(lcb_ref[...], (ts, d_lru))
    for j in range(LRU_CONV_WIDTH):
        xc = xc + _bf16_round(lcw_ref[j:j + 1, :]) * xe_s[pl.ds(HIST_LRU - LRU_BUF + j, ts), :]
    reset = (s * ts + lax.broadcasted_iota(I32, (ts, d_lru), 0)) == 0
    a_t, b_t = _lru_gates(xc, wa_ref, ba_ref[...], wx_ref, bx_ref[...], lam_ref[...], reset)
    hseq = _scan_rows(a_t, b_t, h_s[0:1, :])
    o_lru = hseq * jax.nn.gelu(gl)
    nlc_ref[0] = xr[ts - LRU_BUF:ts, :]
    nlh_ref[0] = hseq[ts - 1:ts, :]
    h_s[...] = jnp.broadcast_to(hseq[ts - 1:ts, :], h_s.shape)
    xe_s[0:HIST_LRU, :] = xe_s[pl.ds(ts, HIST_LRU), :]

    o = jnp.concatenate([o_pool, o_conv, o_lru], axis=1).astype(BF16)
    x1 = x + g1 * jnp.dot(o, wout_ref[...], preferred_element_type=F32)
    x1_ref[...] = x1
    h2 = _rms_mod(x1, n2g_ref[...], sc2, sh2)
    _route(h2, rwt_ref, rb_ref, xl_ref, rt_ref, cnt_ref)


def _mix_sample_kernel(x_ref, mod_ref, n1g_ref, win_ref, poolw_ref, pools_ref, cdw_ref, cb_ref, lng_ref, lnb_ref,
                       lcw_ref, lcb_ref, wa_ref, ba_ref, wx_ref, bx_ref, lam_ref, wout_ref, n2g_ref, rwt_ref,
                       rb_ref, sp_ref, sc_ref, sl_ref, sh_ref,
                       x1_ref, xl_ref, rt_ref, cnt_ref, npool_ref, nconv_ref, nlc_ref, nlh_ref,
                       *, start, d_model, d_pool, d_conv, d_lru):
    n = x_ref.shape[0]
    x = x_ref[...]
    m = mod_ref[...]
    sh1, sc1, g1, sh2, sc2, g2 = [m[:, i * d_model:(i + 1) * d_model] for i in range(6)]
    h = _rms_mod(x, n1g_ref[...], sc1, sh1)
    z = jnp.dot(h.astype(BF16), win_ref[...], preferred_element_type=F32)
    o0 = 0
    u = z[:, o0:o0 + d_pool]
    o0 += d_pool
    v = z[:, o0:o0 + d_conv]
    o0 += d_conv
    gc = z[:, o0:o0 + d_conv]
    o0 += d_conv
    xr = z[:, o0:o0 + d_lru]
    o0 += d_lru
    gl = z[:, o0:o0 + d_lru]

    acc = u
    accs = {}
    for j in range(1, POOL_BUF + 1):
        acc = acc + sp_ref[:, POOL_BUF - j, :]
        if j + 1 in POOL_WINDOWS:
            accs[j + 1] = acc
    lane = lax.broadcasted_iota(I32, (n, d_pool), 1)
    wsum, wlen = _pool_window_select(accs, lane, d_pool // len(POOL_WINDOWS))
    cnt = jnp.minimum(start + 1, wlen).astype(F32)
    dmean = wsum / cnt - u
    o_pool = jnp.dot(dmean.astype(BF16), poolw_ref[...], preferred_element_type=F32) * pools_ref[...]
    for j in range(POOL_BUF - 1):
        npool_ref[:, j, :] = sp_ref[:, j + 1, :]
    npool_ref[:, POOL_BUF - 1, :] = u

    a = v * jax.nn.sigmoid(gc)
    y = cb_ref[...] + cdw_ref[CONV_BUF:CONV_WIDTH, :] * a
    for j in range(CONV_BUF):
        y = y + cdw_ref[j:j + 1, :] * sc_ref[:, j, :]
    o_conv = _layer_norm_silu(y, lng_ref[...], lnb_ref[...])
    for j in range(CONV_BUF - 1):
        nconv_ref[:, j, :] = sc_ref[:, j + 1, :]
    nconv_ref[:, CONV_BUF - 1, :] = a

    xc = lcb_ref[...] + lcw_ref[LRU_BUF:LRU_CONV_WIDTH, :] * xr
    for j in range(LRU_BUF):
        xc = xc + lcw_ref[j:j + 1, :] * sl_ref[:, j, :]
    reset = jnp.full((n, d_lru), True) if start == 0 else None
    a_t, b_t = _lru_gates(xc, wa_ref, ba_ref[...], wx_ref, bx_ref[...], lam_ref[...], reset)
    hnew = a_t * sh_ref[...] + b_t
    o_lru = hnew * jax.nn.gelu(gl)
    for j in range(LRU_BUF - 1):
        nlc_ref[:, j, :] = sl_ref[:, j + 1, :]
    nlc_ref[:, LRU_BUF - 1, :] = xr
    nlh_ref[...] = hnew

    o = jnp.concatenate([o_pool, o_conv, o_lru], axis=1).astype(BF16)
    x1 = x + g1 * jnp.dot(o, wout_ref[...], preferred_element_type=F32)
    x1_ref[...] = x1
    h2 = _rms_mod(x1, n2g_ref[...], sc2, sh2)
    _route(h2, rwt_ref, rb_ref, xl_ref, rt_ref, cnt_ref)


def _adaln_kernel(c_ref, w_ref, b_ref, o_ref):
    c = c_ref[...]
    o_ref[0] = jnp.dot(jax.nn.silu(c).astype(BF16), w_ref[0].astype(BF16), preferred_element_type=F32) + b_ref[0]


def _adaln(c_all, ada_w, ada_b):
    depth, d, n6 = ada_w.shape
    nb = c_all.shape[0]
    tn = d
    return pl.pallas_call(
        _adaln_kernel,
        grid=(depth, n6 // tn),
        in_specs=[pl.BlockSpec((nb, d), lambda l, j: (0, 0)),
                  pl.BlockSpec((1, d, tn), lambda l, j: (l, 0, j)),
                  pl.BlockSpec((1, 1, tn), lambda l, j: (l, 0, j))],
        out_specs=pl.BlockSpec((1, nb, tn), lambda l, j: (l, 0, j)),
        out_shape=jax.ShapeDtypeStruct((depth, nb, n6), F32),
        compiler_params=pltpu.CompilerParams(dimension_semantics=("arbitrary", "arbitrary")),
        name="adaln",
    )(c_all, ada_w, ada_b.reshape(depth, 1, n6))


def _const_spec(shape):
    nd = len(shape)
    return pl.BlockSpec(shape, lambda *_: (0,) * nd)


def _block_diag(w):
    g, a, b = w.shape
    return jnp.einsum("gij,gh->gihj", w, jnp.eye(g, dtype=w.dtype)).reshape(g * a, g * b)


def _layer_weights(p, l):
    def head_tiles(w):
        nh = w.shape[0]
        return jnp.stack([_block_diag(w[:nh // 2]), _block_diag(w[nh // 2:])]).astype(BF16)

    row = lambda a: a[l].reshape(1, -1)
    return [row(p["norm1_g"]), p["w_in"][l].astype(BF16), _block_diag(p["pool_w"][l]).astype(BF16),
            row(p["pool_scale"]),
            p["conv_dw"][l], row(p["conv_b"]), row(p["conv_ln_g"]), row(p["conv_ln_b"]),
            p["lru_conv_w"][l], row(p["lru_conv_b"]), head_tiles(p["lru_wa"][l]), row(p["lru_ba"]),
            head_tiles(p["lru_wx"][l]), row(p["lru_bx"]), row(p["lru_lambda"]),
            p["w_out"][l].astype(BF16), row(p["norm2_g"]), p["router_w"][l].T.astype(BF16),
            p["router_b"][l].reshape(-1, 1)]


def _mix_prompt(x, mod, weights, n_batch, seq):
    t, d_model = x.shape
    ts = SEQ_TILE
    n_s = seq // ts
    d_pool = weights[3].shape[1]
    d_conv = weights[5].shape[1]
    d_lru = weights[9].shape[1]
    n_exp = weights[17].shape[0]
    tile_rows = TOP_K * ts * V7X_SUBLANES
    tok = lambda b, s: (b * n_s + s, 0)
    lanes = lambda b, s: (0, b * n_s + s)
    per_b = lambda b, s: (b, 0, 0)
    in_specs = [pl.BlockSpec((ts, d_model), tok), pl.BlockSpec((1, 6, d_model), per_b)]
    in_specs += [_const_spec(w.shape) for w in weights]
    out_shape = (jax.ShapeDtypeStruct((t, d_model), F32),
                 jax.ShapeDtypeStruct((t * TOP_K * V7X_SUBLANES, V7X_LANES), F32),
                 jax.ShapeDtypeStruct((2 * TOP_K, t), F32),
                 jax.ShapeDtypeStruct((t // ts * n_exp, V7X_LANES), F32),
                 jax.ShapeDtypeStruct((n_batch, POOL_BUF, d_pool), F32),
                 jax.ShapeDtypeStruct((n_batch, CONV_BUF, d_conv), F32),
                 jax.ShapeDtypeStruct((n_batch, LRU_BUF, d_lru), F32),
                 jax.ShapeDtypeStruct((n_batch, 1, d_lru), F32))
    out_specs = (pl.BlockSpec((ts, d_model), tok), pl.BlockSpec((tile_rows, V7X_LANES), tok),
                 pl.BlockSpec((2 * TOP_K, ts), lanes), pl.BlockSpec((n_exp, V7X_LANES), tok),
                 pl.BlockSpec((1, POOL_BUF, d_pool), per_b), pl.BlockSpec((1, CONV_BUF, d_conv), per_b),
                 pl.BlockSpec((1, LRU_BUF, d_lru), per_b), pl.BlockSpec((1, 1, d_lru), per_b))
    scratch = [pltpu.VMEM((HIST_POOL + ts, d_pool), F32), pltpu.VMEM((HIST_CONV + ts, d_conv), F32),
               pltpu.VMEM((HIST_LRU + ts, d_lru), F32), pltpu.VMEM((V7X_SUBLANES, d_lru), F32),
               pltpu.VMEM((V7X_SUBLANES - 1, ts + V7X_SUBLANES, d_conv), F32)]
    return pl.pallas_call(
        functools.partial(_mix_prompt_kernel, ts=ts, d_pool=d_pool, d_conv=d_conv, d_lru=d_lru),
        grid=(n_batch, n_s), in_specs=in_specs, out_specs=out_specs, out_shape=out_shape, scratch_shapes=scratch,
        compiler_params=pltpu.CompilerParams(dimension_semantics=("arbitrary", "arbitrary"),
                                             vmem_limit_bytes=V7X_VMEM_LIMIT_BYTES),
        name="mix_prompt",
    )(x, mod, *weights)


def _mix_sample(x, mod, weights, st_pool, st_conv, st_lruc, st_lruh, start):
    n, d_model = x.shape
    d_pool = weights[3].shape[1]
    d_conv = weights[5].shape[1]
    d_lru = weights[9].shape[1]
    n_exp = weights[17].shape[0]
    ins = [x, mod] + list(weights) + [st_pool, st_conv, st_lruc, st_lruh]
    in_specs = [_const_spec(a.shape) for a in ins]
    shapes = [(n, d_model), (n * TOP_K * V7X_SUBLANES, V7X_LANES), (2 * TOP_K, n), (n_exp, V7X_LANES),
              st_pool.shape, st_conv.shape, st_lruc.shape, st_lruh.shape]
    return pl.pallas_call(
        functools.partial(_mix_sample_kernel, start=start, d_model=d_model, d_pool=d_pool, d_conv=d_conv,
                          d_lru=d_lru),
        grid=(1,), in_specs=in_specs, out_specs=tuple(_const_spec(s) for s in shapes),
        out_shape=tuple(jax.ShapeDtypeStruct(s, F32) for s in shapes),
        compiler_params=pltpu.CompilerParams(dimension_semantics=("arbitrary",),
                                             vmem_limit_bytes=V7X_VMEM_LIMIT_BYTES),
        name="mix_sample",
    )(*ins)


def _rows(ref, row0, n_rows):
    start = pl.multiple_of(row0 * V7X_SUBLANES, V7X_SUBLANES)
    size = pl.multiple_of(n_rows * V7X_SUBLANES, V7X_SUBLANES)
    return ref.at[pl.ds(start, size)]


def _expert_kernel(be_ref, jlo_ref, jhi_ref, nv_ref, na_ref, gdst_ref, glen_ref, gsrc_ref,
                   xlp_ref, xls_ref, wg_ref, wu_ref, wd_ref, bg_ref, bu_ref, bd_ref, ys_ref,
                   xbuf, wbf, gsem, *, bm, runs_per_expert):
    i = pl.program_id(0)
    n_active = na_ref[0]
    cur = i % 2

    def start_gather(blk, buf_slot):
        blk0 = blk * bm
        run0 = be_ref[blk] * runs_per_expert

        def fetch(r, src_ref):
            g0 = gdst_ref[r]
            lo = jnp.maximum(g0, blk0)
            n = jnp.minimum(g0 + glen_ref[r], blk0 + bm) - lo

            @pl.when(n > 0)
            def _():
                pltpu.make_async_copy(_rows(src_ref, gsrc_ref[r] + lo - g0, n),
                                      _rows(xbuf.at[buf_slot], lo - blk0, n), gsem.at[buf_slot]).start()

        def body(j, c):
            fetch(run0 + j, xlp_ref)
            return c
        lax.fori_loop(jlo_ref[blk], jhi_ref[blk], body, 0)
        fetch(run0 + runs_per_expert - 1, xls_ref)

    def wait_gather(blk, buf_slot):
        @pl.when(nv_ref[blk] > 0)
        def _():
            rows = _rows(xbuf.at[buf_slot], 0, nv_ref[blk])
            pltpu.make_async_copy(rows, rows, gsem.at[buf_slot]).wait()

    @pl.when(i == 0)
    def _():
        xbuf[...] = jnp.zeros(xbuf.shape, F32)
        start_gather(0, 0)

    @pl.when(i + 1 < n_active)
    def _():
        start_gather(i + 1, 1 - cur)

    @pl.when(i < n_active)
    def _():
        changed = jnp.logical_or(i == 0, be_ref[i] != be_ref[jnp.maximum(i - 1, 0)])

        @pl.when(changed)
        def _():
            wbf[0] = wg_ref[0, 0].astype(BF16)
            wbf[1] = wu_ref[0, 0].astype(BF16)
            wbf[2] = wd_ref[0, 0].astype(BF16)

        wait_gather(i, cur)
        x = _from_token_tiles(xbuf.at[cur], bm).astype(BF16)
        g = jnp.dot(x, wbf[0], preferred_element_type=F32) + bg_ref[0, 0]
        u = jnp.dot(x, wbf[1], preferred_element_type=F32) + bu_ref[0, 0]
        g = jnp.minimum(g, SWIGLU_LIMIT)
        u = jnp.clip(u, -SWIGLU_LIMIT, SWIGLU_LIMIT)
        act = (u + 1.0) * (g * jax.nn.sigmoid(SWIGLU_ALPHA * g))
        y = jnp.dot(act.astype(BF16), wbf[2], preferred_element_type=F32) + bd_ref[0, 0]
        _to_token_tiles(ys_ref, y)

    @pl.when(i >= n_active)
    def _():
        ys_ref[...] = jnp.zeros(ys_ref.shape, F32)


def _experts(layer, xl_p, xl_s, plan, wg, wu, wd, bg, bu, bd, n_blocks, bm):
    _, n_exp, d, d_ff = wg.shape
    block_e, jlo, jhi, n_valid, n_active, gdst, glen, gsrc = plan
    runs_per_expert = gdst.shape[0] // n_exp
    w_blk = lambda i, be, *_: (layer, be[i], 0, 0)
    b3 = lambda b: b.reshape(b.shape[0], n_exp, 1, b.shape[-1])
    grid_spec = pltpu.PrefetchScalarGridSpec(
        num_scalar_prefetch=8, grid=(n_blocks,),
        in_specs=[pl.BlockSpec(memory_space=pl.ANY), pl.BlockSpec(memory_space=pl.ANY),
                  pl.BlockSpec((1, 1, d, d_ff), w_blk), pl.BlockSpec((1, 1, d, d_ff), w_blk),
                  pl.BlockSpec((1, 1, d_ff, d), w_blk),
                  pl.BlockSpec((1, 1, 1, d_ff), w_blk), pl.BlockSpec((1, 1, 1, d_ff), w_blk),
                  pl.BlockSpec((1, 1, 1, d), w_blk)],
        out_specs=pl.BlockSpec((bm * V7X_SUBLANES, V7X_LANES), lambda i, *_: (i, 0)),
        scratch_shapes=[pltpu.VMEM((2, bm * V7X_SUBLANES, V7X_LANES), F32), pltpu.VMEM((3, d, d_ff), BF16),
                        pltpu.SemaphoreType.DMA((2,))])
    return pl.pallas_call(
        functools.partial(_expert_kernel, bm=bm, runs_per_expert=runs_per_expert),
        grid_spec=grid_spec,
        out_shape=jax.ShapeDtypeStruct((n_blocks * bm * V7X_SUBLANES, V7X_LANES), F32),
        compiler_params=pltpu.CompilerParams(dimension_semantics=("arbitrary",),
                                             vmem_limit_bytes=V7X_VMEM_LIMIT_BYTES),
        name="experts",
    )(block_e, jlo, jhi, n_valid, n_active, gdst, glen, gsrc, xl_p, xl_s, wg, wu, wd, b3(bg), b3(bu), b3(bd))


def _combine_kernel(gdst_ref, glen_ref, off_ref, x1_ref, ys_ref, rt_ref, g2_ref, fg_ref, o_ref, ybuf, sem,
                    *, tile0, n_exp, runs_per_expert, final):
    step = pl.program_id(0)
    n_steps = pl.num_programs(0)
    n = x1_ref.shape[0]
    cur = step % 2

    def start_fetch(tile, buf_slot):
        def body(e, c):
            r = e * runs_per_expert + tile
            ln = glen_ref[r]

            @pl.when(ln > 0)
            def _():
                pltpu.make_async_copy(_rows(ys_ref, gdst_ref[r], ln), _rows(ybuf.at[buf_slot], off_ref[r], ln),
                                      sem.at[buf_slot]).start()
            return c
        lax.fori_loop(0, n_exp, body, 0)

    @pl.when(step == 0)
    def _():
        start_fetch(tile0, 0)

    @pl.when(step + 1 < n_steps)
    def _():
        start_fetch(tile0 + step + 1, 1 - cur)

    pltpu.make_async_copy(ybuf.at[cur], ybuf.at[cur], sem.at[cur]).wait()
    yl = _from_token_tiles(ybuf.at[cur], TOP_K * n)
    rt = rt_ref[...].T
    col = lax.broadcasted_iota(I32, (n, TOP_K * n), 1).astype(F32)
    sel = jnp.zeros((n, TOP_K * n), F32)
    for k in range(TOP_K):
        sel = jnp.where(col == rt[:, k:k + 1], rt[:, TOP_K + k:TOP_K + k + 1], sel)
    y = jnp.dot(sel.astype(BF16), yl.astype(BF16), preferred_element_type=F32)
    x2 = x1_ref[...] + g2_ref[...] * y
    if final:
        ms = jnp.mean(x2 * x2, axis=-1, keepdims=True)
        x2 = x2 * lax.rsqrt(ms + RMS_EPS) * fg_ref[...]
    o_ref[...] = x2


def _combine(x1, ys, rt_tall, g2, final_g, tables, tile0, n_tile, rows_per_gate, final, n_exp):
    t, d = x1.shape
    gdst, glen, off = tables
    runs_per_expert = gdst.shape[0] // n_exp
    if rows_per_gate == 1:
        g2_spec = pl.BlockSpec((n_tile, d), lambda i, *_: (i, 0))
    else:
        g2 = g2.reshape(-1, 1, d)
        g2_spec = pl.BlockSpec((None, 1, d), lambda i, *_: ((i * n_tile) // rows_per_gate, 0, 0))
    grid_spec = pltpu.PrefetchScalarGridSpec(
        num_scalar_prefetch=3, grid=(t // n_tile,),
        in_specs=[pl.BlockSpec((n_tile, d), lambda i, *_: (i, 0)), pl.BlockSpec(memory_space=pl.ANY),
                  pl.BlockSpec((2 * TOP_K, n_tile), lambda i, *_: (0, i)), g2_spec,
                  pl.BlockSpec((1, d), lambda i, *_: (0, 0))],
        out_specs=pl.BlockSpec((n_tile, d), lambda i, *_: (i, 0)),
        scratch_shapes=[pltpu.VMEM((2, TOP_K * n_tile * V7X_SUBLANES, V7X_LANES), F32),
                        pltpu.SemaphoreType.DMA((2,))])
    return pl.pallas_call(
        functools.partial(_combine_kernel, tile0=tile0, n_exp=n_exp, runs_per_expert=runs_per_expert, final=final),
        grid_spec=grid_spec,
        out_shape=jax.ShapeDtypeStruct((t, d), F32),
        compiler_params=pltpu.CompilerParams(dimension_semantics=("arbitrary",),
                                             vmem_limit_bytes=V7X_VMEM_LIMIT_BYTES),
        name="combine",
    )(gdst, glen, off, x1, ys, rt_tall, g2, final_g.reshape(1, d))


def _routing_plan(tile_counts, tile_rows, bm, n_blocks):
    n_tiles, n_exp = tile_counts.shape
    counts = jnp.sum(tile_counts, axis=0)
    pcounts = (counts + bm - 1) // bm * bm
    pends = jnp.cumsum(pcounts)
    pstarts = pends - pcounts
    base = jnp.cumsum(tile_counts, axis=0) - tile_counts
    off = jnp.cumsum(tile_counts, axis=1) - tile_counts
    gdst = (pstarts[None, :] + base).T
    glen = tile_counts.T
    gsrc = (tile_rows[:, None] + off).T
    blk0 = jnp.arange(n_blocks, dtype=I32) * bm
    block_e = jnp.minimum(jnp.sum(pends[None, :] <= blk0[:, None], axis=1), n_exp - 1).astype(I32)
    pick = jax.nn.one_hot(block_e, n_exp, dtype=F32)
    hi = lax.Precision.HIGHEST
    gd_b = jnp.dot(pick, gdst.astype(F32), precision=hi)[:, :n_tiles - 1]
    ge_b = jnp.dot(pick, (gdst + glen).astype(F32), precision=hi)[:, :n_tiles - 1]
    blk0f = blk0.astype(F32)[:, None]
    jlo = jnp.sum(ge_b <= blk0f, axis=1).astype(I32)
    jhi = jnp.sum(gd_b < blk0f + bm, axis=1).astype(I32)
    n_valid = jnp.clip(pstarts[block_e] + counts[block_e] - blk0, 0, bm).astype(I32)
    n_active = (pends[-1] // bm).astype(I32).reshape(1)
    flat = lambda a: a.reshape(-1).astype(I32)
    return ((block_e, jlo, jhi, n_valid, n_active, flat(gdst), flat(glen), flat(gsrc)),
            (flat(gdst), flat(glen), flat(off.T)))


def kernel(x_prompt, x_sample, state_pool, state_conv, state_lru_conv, state_lru_h, c_prompt, c_sample, ada_w, ada_b, norm1_g, w_in, pool_w, pool_scale, conv_dw, conv_b, conv_ln_g, conv_ln_b, lru_conv_w, lru_conv_b, lru_wa, lru_ba, lru_wx, lru_bx, lru_lambda, w_out, norm2_g, router_w, router_b, moe_w_gate, moe_b_gate, moe_w_up, moe_b_up, moe_w_down, moe_b_down, final_g):
    p = dict(norm1_g=norm1_g, w_in=w_in, pool_w=pool_w, pool_scale=pool_scale, conv_dw=conv_dw, conv_b=conv_b,
             conv_ln_g=conv_ln_g, conv_ln_b=conv_ln_b, lru_conv_w=lru_conv_w, lru_conv_b=lru_conv_b, lru_wa=lru_wa,
             lru_ba=lru_ba, lru_wx=lru_wx, lru_bx=lru_bx, lru_lambda=lru_lambda, w_out=w_out, norm2_g=norm2_g,
             router_w=router_w, router_b=router_b)
    n_b, seq, d_model = x_prompt.shape
    n_s = x_sample.shape[0]
    depth = ada_w.shape[0]
    n_exp = router_w.shape[2]
    t_p = n_b * seq
    t_all = t_p + n_s
    ts = SEQ_TILE
    n_tiles_p = t_p // ts
    bm = MOE_BLOCK
    n_blocks = -(-(t_all * TOP_K) // bm) + n_exp
    tile_rows = jnp.concatenate([jnp.arange(n_tiles_p, dtype=I32) * (TOP_K * ts), jnp.zeros((1,), I32)])

    mod = _adaln(jnp.concatenate([c_prompt, c_sample], axis=0), ada_w, ada_b)
    xp = x_prompt.reshape(t_p, d_model)
    xs = x_sample.reshape(n_s, d_model)

    new_p, new_s = [], []
    for l in range(depth):
        weights = _layer_weights(p, l)
        mod_p = mod[l, :n_b].reshape(n_b, 6, d_model)
        mod_s = mod[l, n_b:]
        x1p, xl_p, rt_p, cnt_p, npool, nconv, nlc, nlh = _mix_prompt(xp, mod_p, weights, n_b, seq)
        x1s, xl_s, rt_s, cnt_s, spool, sconv, slc, slh = _mix_sample(
            xs, mod_s, weights, state_pool[l], state_conv[l], state_lru_conv[l], state_lru_h[l], PAST_LEN)
        new_p.append((npool, nconv, nlc, nlh.reshape(n_b, -1)))
        new_s.append((spool, sconv, slc, slh))

        tile_counts = jnp.concatenate([cnt_p[:, 0].reshape(n_tiles_p, n_exp), cnt_s[:, 0].reshape(1, n_exp)],
                                      axis=0).astype(I32)
        plan, tables = _routing_plan(tile_counts, tile_rows, bm, n_blocks)
        ys = _experts(l, xl_p, xl_s, plan, moe_w_gate, moe_w_up, moe_w_down, moe_b_gate, moe_b_up, moe_b_down,
                      n_blocks, bm)
        final = l == depth - 1
        g2_p = mod[l, :n_b, 5 * d_model:]
        g2_s = mod[l, n_b:, 5 * d_model:]
        xp = _combine(x1p, ys, rt_p, g2_p, final_g, tables, 0, ts, seq, final, n_exp)
        xs = _combine(x1s, ys, rt_s, g2_s, final_g, tables, n_tiles_p, n_s, 1, final, n_exp)

    stack = lambda items, i: jnp.stack([it[i] for it in items])
    return (xp.reshape(n_b, seq, d_model), xs.reshape(n_s, 1, d_model),
            stack(new_p, 0), stack(new_p, 1), stack(new_p, 2), stack(new_p, 3),
            stack(new_s, 0), stack(new_s, 1), stack(new_s, 2), stack(new_s, 3))
```

```python
import functools

import jax
import jax.numpy as jnp
from jax import lax
from jax.experimental import pallas as pl
from jax.experimental.pallas import tpu as pltpu

F32 = jnp.float32
BF16 = jnp.bfloat16
I32 = jnp.int32

POOL_WINDOWS = (2, 4, 8, 16)
POOL_BUF = max(POOL_WINDOWS) - 1
CONV_WIDTH = 31
CONV_BUF = CONV_WIDTH - 1
LRU_CONV_WIDTH = 4
LRU_BUF = LRU_CONV_WIDTH - 1
LRU_C = 8.0
TOP_K = 4
SWIGLU_ALPHA = 1.702
SWIGLU_LIMIT = 7.0
RMS_EPS = 1e-6
LN_EPS = 1e-5
PAST_LEN = 16384

V7X_SUBLANES = 8
V7X_LANES = 128
V7X_VMEM_LIMIT_BYTES = 56 * 1024 * 1024

SEQ_TILE = 256
MOE_BLOCK = 512
HIST_POOL = 16
HIST_CONV = 32
HIST_LRU = 8


def _bf16_round(x):
    return x.astype(BF16).astype(F32)


def _rms_mod(x, g, scale, shift):
    ms = jnp.mean(x * x, axis=-1, keepdims=True)
    return x * lax.rsqrt(ms + RMS_EPS) * g * (1.0 + scale) + shift


def _pool_window_select(accs, lane, pool_group):
    n = len(POOL_WINDOWS)
    wsum = accs[POOL_WINDOWS[-1]]
    wlen = jnp.full(lane.shape, POOL_WINDOWS[-1], I32)
    for g in reversed(range(n - 1)):
        m = lane < (g + 1) * pool_group
        wsum = jnp.where(m, accs[POOL_WINDOWS[g]], wsum)
        wlen = jnp.where(m, POOL_WINDOWS[g], wlen)
    return wsum, wlen


def _layer_norm_silu(y, g, b):
    mu = jnp.mean(y, axis=-1, keepdims=True)
    yc = y - mu
    var = jnp.mean(yc * yc, axis=-1, keepdims=True)
    return jax.nn.silu(yc * lax.rsqrt(var + LN_EPS) * g + b)


def _softplus(x):
    return jnp.maximum(x, 0.0) + jnp.log1p(jnp.exp(-jnp.abs(x)))


def _lru_gates(xc, wa_ref, ba, wx_ref, bx, lam, reset):
    half = xc.shape[1] // 2
    xb = xc.astype(BF16)

    def heads(w_ref):
        lo = jnp.dot(xb[:, :half], w_ref[0], preferred_element_type=F32)
        hi = jnp.dot(xb[:, half:], w_ref[1], preferred_element_type=F32)
        return jnp.concatenate([lo, hi], axis=1)

    r = jax.nn.sigmoid(heads(wa_ref) + ba)
    i = jax.nn.sigmoid(heads(wx_ref) + bx)
    log_a = -LRU_C * r * _softplus(-lam)
    a = jnp.exp(log_a)
    t = 1.0 - a * a
    mult = jnp.where(t > 0.0, t * lax.rsqrt(t), 0.0)
    if reset is not None:
        a = jnp.where(reset, 0.0, a)
        mult = jnp.where(reset, 1.0, mult)
    return a, mult * i * xc


def _scan_rows(a, b, h0):
    rows, c = a.shape
    groups = rows // V7X_SUBLANES
    a3 = a.reshape(groups, V7X_SUBLANES, c)
    b3 = b.reshape(groups, V7X_SUBLANES, c)
    sub = lax.broadcasted_iota(I32, a3.shape, 1)
    d = 1
    while d < V7X_SUBLANES:
        ar = pltpu.roll(a3, d, axis=1)
        br = pltpu.roll(b3, d, axis=1)
        m = sub >= d
        b3 = jnp.where(m, a3 * br + b3, b3)
        a3 = jnp.where(m, a3 * ar, a3)
        d *= 2
    hp = h0
    outs = []
    for g in range(groups):
        hg = a3[g] * hp + b3[g]
        outs.append(hg)
        hp = hg[V7X_SUBLANES - 1:V7X_SUBLANES, :]
    return jnp.concatenate(outs, axis=0)


def _causal_taps(ext_ref, w_ref, bias, first_row, n_out, tmp_ref):
    n_taps = w_ref.shape[0]
    y = jnp.broadcast_to(bias, (n_out, ext_ref.shape[1]))
    for r in range(V7X_SUBLANES):
        rows = n_out if r == 0 else n_out + V7X_SUBLANES
        part = None
        for j in range(n_taps):
            if (first_row + j) % V7X_SUBLANES == r:
                term = w_ref[j:j + 1, :] * ext_ref[pl.ds(first_row + j - r, rows), :]
                part = term if part is None else part + term
        if part is None:
            continue
        if r == 0:
            y = y + part
        else:
            tmp_ref[r - 1] = part
            y = y + tmp_ref[r - 1, pl.ds(r, n_out), :]
    return y


def _to_token_tiles(ref, x):
    n, d = x.shape
    for s in range(d // V7X_LANES):
        ref[pl.ds(s, n, stride=V7X_SUBLANES), :] = x[:, s * V7X_LANES:(s + 1) * V7X_LANES]


def _from_token_tiles(ref, n):
    return jnp.concatenate([ref[pl.ds(s, n, stride=V7X_SUBLANES), :] for s in range(V7X_SUBLANES)], axis=1)


def _route(h2, rwt_ref, rb_ref, xl_ref, rt_ref, cnt_ref):
    n = h2.shape[0]
    n_exp = rwt_ref.shape[0]
    logits = lax.dot_general(rwt_ref[...], h2.astype(BF16), (((1,), (1,)), ((), ())),
                             preferred_element_type=F32) + rb_ref[...]
    eio = lax.broadcasted_iota(I32, (n_exp, n), 0).astype(F32)
    l = logits
    vals, sels = [], []
    for _ in range(TOP_K):
        m = jnp.max(l, axis=0, keepdims=True)
        idx = jnp.min(jnp.where(l == m, eio, float(n_exp)), axis=0, keepdims=True)
        sel = eio == idx
        vals.append(m)
        sels.append(sel)
        l = jnp.where(sel, -jnp.inf, l)
    exps = [jnp.exp(v - vals[0]) for v in vals]
    den = exps[0]
    for e in exps[1:]:
        den = den + e
    ws = [e / den for e in exps]
    onehot = jnp.zeros((n_exp, n), F32)
    for sel in sels:
        onehot = jnp.where(sel, 1.0, onehot)
    earlier = lax.broadcasted_iota(I32, (n, n), 0) < lax.broadcasted_iota(I32, (n, n), 1)
    prior = jnp.dot(onehot.astype(BF16), jnp.where(earlier, 1.0, 0.0).astype(BF16), preferred_element_type=F32)
    cnt = jnp.broadcast_to(jnp.sum(onehot, axis=1, keepdims=True), (n_exp, V7X_LANES))
    lower = lax.broadcasted_iota(I32, (n_exp, n_exp), 0) > lax.broadcasted_iota(I32, (n_exp, n_exp), 1)
    off = jnp.dot(jnp.where(lower, 1.0, 0.0), cnt, precision=lax.Precision.HIGHEST, preferred_element_type=F32)
    pos = prior + jnp.concatenate([off] * (n // V7X_LANES), axis=1)
    rhos = [jnp.sum(jnp.where(sel, pos, 0.0), axis=0, keepdims=True) for sel in sels]
    rt_ref[...] = jnp.concatenate(rhos + ws, axis=0)
    cnt_ref[...] = cnt
    rows = lax.broadcasted_iota(I32, (TOP_K * n, n), 0).astype(F32)
    perm = jnp.zeros((TOP_K * n, n), F32)
    for rho in rhos:
        perm = jnp.where(rows == rho, 1.0, perm)
    xl = jnp.dot(perm.astype(BF16), h2.astype(BF16), preferred_element_type=F32)
    _to_token_tiles(xl_ref, xl)


def _mix_prompt_kernel(x_ref, mod_ref, n1g_ref, win_ref, poolw_ref, pools_ref, cdw_ref, cb_ref, lng_ref, lnb_ref,
                       lcw_ref, lcb_ref, wa_ref, ba_ref, wx_ref, bx_ref, lam_ref, wout_ref, n2g_ref, rwt_ref,
                       rb_ref,
                       x1_ref, xl_ref, rt_ref, cnt_ref, npool_ref, nconv_ref, nlc_ref, nlh_ref,
                       e_s, a_s, xe_s, h_s, t_s, *, ts, d_pool, d_conv, d_lru):
    s = pl.program_id(1)

    @pl.when(s == 0)
    def _():
        e_s[0:HIST_POOL, :] = jnp.zeros((HIST_POOL, d_pool), F32)
        a_s[0:HIST_CONV, :] = jnp.zeros((HIST_CONV, d_conv), F32)
        xe_s[0:HIST_LRU, :] = jnp.zeros((HIST_LRU, d_lru), F32)
        h_s[...] = jnp.zeros(h_s.shape, F32)

    x = x_ref[...]
    m = mod_ref[0]
    sh1, sc1, g1, sh2, sc2, g2 = [m[i:i + 1] for i in range(6)]
    h = _rms_mod(x, n1g_ref[...], sc1, sh1)
    z = jnp.dot(h.astype(BF16), win_ref[...], preferred_element_type=F32)
    o0 = 0
    u = z[:, o0:o0 + d_pool]
    o0 += d_pool
    v = z[:, o0:o0 + d_conv]
    o0 += d_conv
    gc = z[:, o0:o0 + d_conv]
    o0 += d_conv
    xr = z[:, o0:o0 + d_lru]
    o0 += d_lru
    gl = z[:, o0:o0 + d_lru]

    e_s[HIST_POOL:HIST_POOL + ts, :] = u
    acc = u
    accs = {}
    for j in range(1, POOL_BUF + 1):
        acc = acc + e_s[pl.ds(HIST_POOL - j, ts), :]
        if j + 1 in POOL_WINDOWS:
            accs[j + 1] = acc
    lane = lax.broadcasted_iota(I32, (ts, d_pool), 1)
    pos = s * ts + lax.broadcasted_iota(I32, (ts, d_pool), 0)
    wsum, wlen = _pool_window_select(accs, lane, d_pool // len(POOL_WINDOWS))
    cnt = jnp.minimum(pos + 1, wlen).astype(F32)
    dmean = wsum / cnt - u
    o_pool = jnp.dot(dmean.astype(BF16), poolw_ref[...], preferred_element_type=F32) * pools_ref[...]
    npool_ref[0] = e_s[pl.ds(HIST_POOL + ts - POOL_BUF, POOL_BUF), :]
    e_s[0:HIST_POOL, :] = e_s[pl.ds(ts, HIST_POOL), :]

    a = v * jax.nn.sigmoid(gc)
    a_s[HIST_CONV:HIST_CONV + ts, :] = a
    y = _causal_taps(a_s, cdw_ref, cb_ref[...], HIST_CONV - CONV_BUF, ts, t_s)
    o_conv = _layer_norm_silu(y, lng_ref[...], lnb_ref[...])
    nconv_ref[0] = a[ts - CONV_BUF:ts, :]
    a_s[0:HIST_CONV, :] = a_s[pl.ds(ts, HIST_CONV), :]

    xe_s[HIST_LRU:HIST_LRU + ts, :] = _bf16_round(xr)
    xc = jnp.broadcast_to(lcb_ref[...], (ts, d_lru))
    for j in range(LRU_CONV_WIDTH):
        xc = xc + _bf16_round(lcw_ref[j:j + 1, :]) * xe_s[pl.ds(HIST_LRU - LRU_BUF + j, ts), :]
    reset = (s * ts + lax.broadcasted_iota(I32, (ts, d_lru), 0)) == 0
    a_t, b_t = _lru_gates(xc, wa_ref, ba_ref[...], wx_ref, bx_ref[...], lam_ref[...], reset)
    hseq = _scan_rows(a_t, b_t, h_s[0:1, :])
    o_lru = hseq * jax.nn.gelu(gl)
    nlc_ref[0] = xr[ts - LRU_BUF:ts, :]
    nlh_ref[0] = hseq[ts - 1:ts, :]
    h_s[...] = jnp.broadcast_to(hseq[ts - 1:ts, :], h_s.shape)
    xe_s[0:HIST_LRU, :] = xe_s[pl.ds(ts, HIST_LRU), :]

    o = jnp.concatenate([o_pool, o_conv, o_lru], axis=1).astype(BF16)
    x1 = x + g1 * jnp.dot(o, wout_ref[...], preferred_element_type=F32)
    x1_ref[...] = x1
    h2 = _rms_mod(x1, n2g_ref[...], sc2, sh2)
    _route(h2, rwt_ref, rb_ref, xl_ref, rt_ref, cnt_ref)


def _mix_sample_kernel(x_ref, mod_ref, n1g_ref, win_ref, poolw_ref, pools_ref, cdw_ref, cb_ref, lng_ref, lnb_ref,
                       lcw_ref, lcb_ref, wa_ref, ba_ref, wx_ref, bx_ref, lam_ref, wout_ref, n2g_ref, rwt_ref,
                       rb_ref, sp_ref, sc_ref, sl_ref, sh_ref,
                       x1_ref, xl_ref, rt_ref, cnt_ref, npool_ref, nconv_ref, nlc_ref, nlh_ref,
                       *, start, d_model, d_pool, d_conv, d_lru):
    n = x_ref.shape[0]
    x = x_ref[...]
    m = mod_ref[...]
    sh1, sc1, g1, sh2, sc2, g2 = [m[:, i * d_model:(i + 1) * d_model] for i in range(6)]
    h = _rms_mod(x, n1g_ref[...], sc1, sh1)
    z = jnp.dot(h.astype(BF16), win_ref[...], preferred_element_type=F32)
    o0 = 0
    u = z[:, o0:o0 + d_pool]
    o0 += d_pool
    v = z[:, o0:o0 + d_conv]
    o0 += d_conv
    gc = z[:, o0:o0 + d_conv]
    o0 += d_conv
    xr = z[:, o0:o0 + d_lru]
    o0 += d_lru
    gl = z[:, o0:o0 + d_lru]

    acc = u
    accs = {}
    for j in range(1, POOL_BUF + 1):
        acc = acc + sp_ref[:, POOL_BUF - j, :]
        if j + 1 in POOL_WINDOWS:
            accs[j + 1] = acc
    lane = lax.broadcasted_iota(I32, (n, d_pool), 1)
    wsum, wlen = _pool_window_select(accs, lane, d_pool // len(POOL_WINDOWS))
    cnt = jnp.minimum(start + 1, wlen).astype(F32)
    dmean = wsum / cnt - u
    o_pool = jnp.dot(dmean.astype(BF16), poolw_ref[...], preferred_element_type=F32) * pools_ref[...]
    for j in range(POOL_BUF - 1):
        npool_ref[:, j, :] = sp_ref[:, j + 1, :]
    npool_ref[:, POOL_BUF - 1, :] = u

    a = v * jax.nn.sigmoid(gc)
    y = cb_ref[...] + cdw_ref[CONV_BUF:CONV_WIDTH, :] * a
    for j in range(CONV_BUF):
        y = y + cdw_ref[j:j + 1, :] * sc_ref[:, j, :]
    o_conv = _layer_norm_silu(y, lng_ref[...], lnb_ref[...])
    for j in range(CONV_BUF - 1):
        nconv_ref[:, j, :] = sc_ref[:, j + 1, :]
    nconv_ref[:, CONV_BUF - 1, :] = a

    xc = lcb_ref[...] + lcw_ref[LRU_BUF:LRU_CONV_WIDTH, :] * xr
    for j in range(LRU_BUF):
        xc = xc + lcw_ref[j:j + 1, :] * sl_ref[:, j, :]
    reset = jnp.full((n, d_lru), True) if start == 0 else None
    a_t, b_t = _lru_gates(xc, wa_ref, ba_ref[...], wx_ref, bx_ref[...], lam_ref[...], reset)
    hnew = a_t * sh_ref[...] + b_t
    o_lru = hnew * jax.nn.gelu(gl)
    for j in range(LRU_BUF - 1):
        nlc_ref[:, j, :] = sl_ref[:, j + 1, :]
    nlc_ref[:, LRU_BUF - 1, :] = xr
    nlh_ref[...] = hnew

    o = jnp.concatenate([o_pool, o_conv, o_lru], axis=1).astype(BF16)
    x1 = x + g1 * jnp.dot(o, wout_ref[...], preferred_element_type=F32)
    x1_ref[...] = x1
    h2 = _rms_mod(x1, n2g_ref[...], sc2, sh2)
    _route(h2, rwt_ref, rb_ref, xl_ref, rt_ref, cnt_ref)


def _adaln_kernel(c_ref, w_ref, b_ref, o_ref):
    c = c_ref[...]
    o_ref[0] = jnp.dot(jax.nn.silu(c).astype(BF16), w_ref[0].astype(BF16), preferred_element_type=F32) + b_ref[0]


def _adaln(c_all, ada_w, ada_b):
    depth, d, n6 = ada_w.shape
    nb = c_all.shape[0]
    tn = d
    return pl.pallas_call(
        _adaln_kernel,
        grid=(depth, n6 // tn),
        in_specs=[pl.BlockSpec((nb, d), lambda l, j: (0, 0)),
                  pl.BlockSpec((1, d, tn), lambda l, j: (l, 0, j)),
                  pl.BlockSpec((1, 1, tn), lambda l, j: (l, 0, j))],
        out_specs=pl.BlockSpec((1, nb, tn), lambda l, j: (l, 0, j)),
        out_shape=jax.ShapeDtypeStruct((depth, nb, n6), F32),
        compiler_params=pltpu.CompilerParams(dimension_semantics=("arbitrary", "arbitrary")),
        name="adaln",
    )(c_all, ada_w, ada_b.reshape(depth, 1, n6))


def _const_spec(shape):
    nd = len(shape)
    return pl.BlockSpec(shape, lambda *_: (0,) * nd)


def _block_diag(w):
    g, a, b = w.shape
    return jnp.einsum("gij,gh->gihj", w, jnp.eye(g, dtype=w.dtype)).reshape(g * a, g * b)


def _layer_weights(p, l):
    def head_tiles(w):
        nh = w.shape[0]
        return jnp.stack([_block_diag(w[:nh // 2]), _block_diag(w[nh // 2:])]).astype(BF16)

    row = lambda a: a[l].reshape(1, -1)
    return [row(p["norm1_g"]), p["w_in"][l].astype(BF16), _block_diag(p["pool_w"][l]).astype(BF16),
            row(p["pool_scale"]),
            p["conv_dw"][l], row(p["conv_b"]), row(p["conv_ln_g"]), row(p["conv_ln_b"]),
            p["lru_conv_w"][l], row(p["lru_conv_b"]), head_tiles(p["lru_wa"][l]), row(p["lru_ba"]),
            head_tiles(p["lru_wx"][l]), row(p["lru_bx"]), row(p["lru_lambda"]),
            p["w_out"][l].astype(BF16), row(p["norm2_g"]), p["router_w"][l].T.astype(BF16),
            p["router_b"][l].reshape(-1, 1)]


def _mix_prompt(x, mod, weights, n_batch, seq):
    t, d_model = x.shape
    ts = SEQ_TILE
    n_s = seq // ts
    d_pool = weights[3].shape[1]
    d_conv = weights[5].shape[1]
    d_lru = weights[9].shape[1]
    n_exp = weights[17].shape[0]
    tile_rows = TOP_K * ts * V7X_SUBLANES
    tok = lambda b, s: (b * n_s + s, 0)
    lanes = lambda b, s: (0, b * n_s + s)
    per_b = lambda b, s: (b, 0, 0)
    in_specs = [pl.BlockSpec((ts, d_model), tok), pl.BlockSpec((1, 6, d_model), per_b)]
    in_specs += [_const_spec(w.shape) for w in weights]
    out_shape = (jax.ShapeDtypeStruct((t, d_model), F32),
                 jax.ShapeDtypeStruct((t * TOP_K * V7X_SUBLANES, V7X_LANES), F32),
                 jax.ShapeDtypeStruct((2 * TOP_K, t), F32),
                 jax.ShapeDtypeStruct((t // ts * n_exp, V7X_LANES), F32),
                 jax.ShapeDtypeStruct((n_batch, POOL_BUF, d_pool), F32),
                 jax.ShapeDtypeStruct((n_batch, CONV_BUF, d_conv), F32),
                 jax.ShapeDtypeStruct((n_batch, LRU_BUF, d_lru), F32),
                 jax.ShapeDtypeStruct((n_batch, 1, d_lru), F32))
    out_specs = (pl.BlockSpec((ts, d_model), tok), pl.BlockSpec((tile_rows, V7X_LANES), tok),
                 pl.BlockSpec((2 * TOP_K, ts), lanes), pl.BlockSpec((n_exp, V7X_LANES), tok),
                 pl.BlockSpec((1, POOL_BUF, d_pool), per_b), pl.BlockSpec((1, CONV_BUF, d_conv), per_b),
                 pl.BlockSpec((1, LRU_BUF, d_lru), per_b), pl.BlockSpec((1, 1, d_lru), per_b))
    scratch = [pltpu.VMEM((HIST_POOL + ts, d_pool), F32), pltpu.VMEM((HIST_CONV + ts, d_conv), F32),
               pltpu.VMEM((HIST_LRU + ts, d_lru), F32), pltpu.VMEM((V7X_SUBLANES, d_lru), F32),
               pltpu.VMEM((V7X_SUBLANES - 1, ts + V7X_SUBLANES, d_conv), F32)]
    return pl.pallas_call(
        functools.partial(_mix_prompt_kernel, ts=ts, d_pool=d_pool, d_conv=d_conv, d_lru=d_lru),
        grid=(n_batch, n_s), in_specs=in_specs, out_specs=out_specs, out_shape=out_shape, scratch_shapes=scratch,
        compiler_params=pltpu.CompilerParams(dimension_semantics=("arbitrary", "arbitrary"),
                                             vmem_limit_bytes=V7X_VMEM_LIMIT_BYTES),
        name="mix_prompt",
    )(x, mod, *weights)


def _mix_sample(x, mod, weights, st_pool, st_conv, st_lruc, st_lruh, start):
    n, d_model = x.shape
    d_pool = weights[3].shape[1]
    d_conv = weights[5].shape[1]
    d_lru = weights[9].shape[1]
    n_exp = weights[17].shape[0]
    ins = [x, mod] + list(weights) + [st_pool, st_conv, st_lruc, st_lruh]
    in_specs = [_const_spec(a.shape) for a in ins]
    shapes = [(n, d_model), (n * TOP_K * V7X_SUBLANES, V7X_LANES), (2 * TOP_K, n), (n_exp, V7X_LANES),
              st_pool.shape, st_conv.shape, st_lruc.shape, st_lruh.shape]
    return pl.pallas_call(
        functools.partial(_mix_sample_kernel, start=start, d_model=d_model, d_pool=d_pool, d_conv=d_conv,
                          d_lru=d_lru),
        grid=(1,), in_specs=in_specs, out_specs=tuple(_const_spec(s) for s in shapes),
        out_shape=tuple(jax.ShapeDtypeStruct(s, F32) for s in shapes),
        compiler_params=pltpu.CompilerParams(dimension_semantics=("arbitrary",),
                                             vmem_limit_bytes=V7X_VMEM_LIMIT_BYTES),
        name="mix_sample",
    )(*ins)


def _rows(ref, row0, n_rows):
    start = pl.multiple_of(row0 * V7X_SUBLANES, V7X_SUBLANES)
    size = pl.multiple_of(n_rows * V7X_SUBLANES, V7X_SUBLANES)
    return ref.at[pl.ds(start, size)]


def _expert_kernel(be_ref, jlo_ref, jhi_ref, nv_ref, na_ref, gdst_ref, glen_ref, gsrc_ref,
                   xlp_ref, xls_ref, wg_ref, wu_ref, wd_ref, bg_ref, bu_ref, bd_ref, ys_ref,
                   xbuf, wbf, gsem, *, bm, runs_per_expert):
    i = pl.program_id(0)
    n_active = na_ref[0]
    cur = i % 2

    def start_gather(blk, buf_slot):
        blk0 = blk * bm
        run0 = be_ref[blk] * runs_per_expert

        def fetch(r, src_ref):
            g0 = gdst_ref[r]
            lo = jnp.maximum(g0, blk0)
            n = jnp.minimum(g0 + glen_ref[r], blk0 + bm) - lo

            @pl.when(n > 0)
            def _():
                pltpu.make_async_copy(_rows(src_ref, gsrc_ref[r] + lo - g0, n),
                                      _rows(xbuf.at[buf_slot], lo - blk0, n), gsem.at[buf_slot]).start()

        def body(j, c):
            fetch(run0 + j, xlp_ref)
            return c
        lax.fori_loop(jlo_ref[blk], jhi_ref[blk], body, 0)
        fetch(run0 + runs_per_expert - 1, xls_ref)

    def wait_gather(blk, buf_slot):
        @pl.when(nv_ref[blk] > 0)
        def _():
            rows = _rows(xbuf.at[buf_slot], 0, nv_ref[blk])
            pltpu.make_async_copy(rows, rows, gsem.at[buf_slot]).wait()

    @pl.when(i == 0)
    def _():
        xbuf[...] = jnp.zeros(xbuf.shape, F32)
        start_gather(0, 0)

    @pl.when(i + 1 < n_active)
    def _():
        start_gather(i + 1, 1 - cur)

    @pl.when(i < n_active)
    def _():
        changed = jnp.logical_or(i == 0, be_ref[i] != be_ref[jnp.maximum(i - 1, 0)])

        @pl.when(changed)
        def _():
            wbf[0] = wg_ref[0, 0].astype(BF16)
            wbf[1] = wu_ref[0, 0].astype(BF16)
            wbf[2] = wd_ref[0, 0].astype(BF16)

        wait_gather(i, cur)
        x = _from_token_tiles(xbuf.at[cur], bm).astype(BF16)
        g = jnp.dot(x, wbf[0], preferred_element_type=F32) + bg_ref[0, 0]
        u = jnp.dot(x, wbf[1], preferred_element_type=F32) + bu_ref[0, 0]
        g = jnp.minimum(g, SWIGLU_LIMIT)
        u = jnp.clip(u, -SWIGLU_LIMIT, SWIGLU_LIMIT)
        act = (u + 1.0) * (g * jax.nn.sigmoid(SWIGLU_ALPHA * g))
        y = jnp.dot(act.astype(BF16), wbf[2], preferred_element_type=F32) + bd_ref[0, 0]
        _to_token_tiles(ys_ref, y)

    @pl.when(i >= n_active)
    def _():
        ys_ref[...] = jnp.zeros(ys_ref.shape, F32)


def _experts(layer, xl_p, xl_s, plan, wg, wu, wd, bg, bu, bd, n_blocks, bm):
    _, n_exp, d, d_ff = wg.shape
    block_e, jlo, jhi, n_valid, n_active, gdst, glen, gsrc = plan
    runs_per_expert = gdst.shape[0] // n_exp
    w_blk = lambda i, be, *_: (layer, be[i], 0, 0)
    b3 = lambda b: b.reshape(b.shape[0], n_exp, 1, b.shape[-1])
    grid_spec = pltpu.PrefetchScalarGridSpec(
        num_scalar_prefetch=8, grid=(n_blocks,),
        in_specs=[pl.BlockSpec(memory_space=pl.ANY), pl.BlockSpec(memory_space=pl.ANY),
                  pl.BlockSpec((1, 1, d, d_ff), w_blk), pl.BlockSpec((1, 1, d, d_ff), w_blk),
                  pl.BlockSpec((1, 1, d_ff, d), w_blk),
                  pl.BlockSpec((1, 1, 1, d_ff), w_blk), pl.BlockSpec((1, 1, 1, d_ff), w_blk),
                  pl.BlockSpec((1, 1, 1, d), w_blk)],
        out_specs=pl.BlockSpec((bm * V7X_SUBLANES, V7X_LANES), lambda i, *_: (i, 0)),
        scratch_shapes=[pltpu.VMEM((2, bm * V7X_SUBLANES, V7X_LANES), F32), pltpu.VMEM((3, d, d_ff), BF16),
                        pltpu.SemaphoreType.DMA((2,))])
    return pl.pallas_call(
        functools.partial(_expert_kernel, bm=bm, runs_per_expert=runs_per_expert),
        grid_spec=grid_spec,
        out_shape=jax.ShapeDtypeStruct((n_blocks * bm * V7X_SUBLANES, V7X_LANES), F32),
        compiler_params=pltpu.CompilerParams(dimension_semantics=("arbitrary",),
                                             vmem_limit_bytes=V7X_VMEM_LIMIT_BYTES),
        name="experts",
    )(block_e, jlo, jhi, n_valid, n_active, gdst, glen, gsrc, xl_p, xl_s, wg, wu, wd, b3(bg), b3(bu), b3(bd))


def _combine_kernel(gdst_ref, glen_ref, off_ref, x1_ref, ys_ref, rt_ref, g2_ref, fg_ref, o_ref, ybuf, sem,
                    *, tile0, n_exp, runs_per_expert, final):
    step = pl.program_id(0)
    n_steps = pl.num_programs(0)
    n = x1_ref.shape[0]
    cur = step % 2

    def start_fetch(tile, buf_slot):
        def body(e, c):
            r = e * runs_per_expert + tile
            ln = glen_ref[r]

            @pl.when(ln > 0)
            def _():
                pltpu.make_async_copy(_rows(ys_ref, gdst_ref[r], ln), _rows(ybuf.at[buf_slot], off_ref[r], ln),
                                      sem.at[buf_slot]).start()
            return c
        lax.fori_loop(0, n_exp, body, 0)

    @pl.when(step == 0)
    def _():
        start_fetch(tile0, 0)

    @pl.when(step + 1 < n_steps)
    def _():
        start_fetch(tile0 + step + 1, 1 - cur)

    pltpu.make_async_copy(ybuf.at[cur], ybuf.at[cur], sem.at[cur]).wait()
    yl = _from_token_tiles(ybuf.at[cur], TOP_K * n)
    rt = rt_ref[...].T
    col = lax.broadcasted_iota(I32, (n, TOP_K * n), 1).astype(F32)
    sel = jnp.zeros((n, TOP_K * n), F32)
    for k in range(TOP_K):
        sel = jnp.where(col == rt[:, k:k + 1], rt[:, TOP_K + k:TOP_K + k + 1], sel)
    y = jnp.dot(sel.astype(BF16), yl.astype(BF16), preferred_element_type=F32)
    x2 = x1_ref[...] + g2_ref[...] * y
    if final:
        ms = jnp.mean(x2 * x2, axis=-1, keepdims=True)
        x2 = x2 * lax.rsqrt(ms + RMS_EPS) * fg_ref[...]
    o_ref[...] = x2


def _combine(x1, ys, rt_tall, g2, final_g, tables, tile0, n_tile, rows_per_gate, final, n_exp):
    t, d = x1.shape
    gdst, glen, off = tables
    runs_per_expert = gdst.shape[0] // n_exp
    if rows_per_gate == 1:
        g2_spec = pl.BlockSpec((n_tile, d), lambda i, *_: (i, 0))
    else:
        g2 = g2.reshape(-1, 1, d)
        g2_spec = pl.BlockSpec((None, 1, d), lambda i, *_: ((i * n_tile) // rows_per_gate, 0, 0))
    grid_spec = pltpu.PrefetchScalarGridSpec(
        num_scalar_prefetch=3, grid=(t // n_tile,),
        in_specs=[pl.BlockSpec((n_tile, d), lambda i, *_: (i, 0)), pl.BlockSpec(memory_space=pl.ANY),
                  pl.BlockSpec((2 * TOP_K, n_tile), lambda i, *_: (0, i)), g2_spec,
                  pl.BlockSpec((1, d), lambda i, *_: (0, 0))],
        out_specs=pl.BlockSpec((n_tile, d), lambda i, *_: (i, 0)),
        scratch_shapes=[pltpu.VMEM((2, TOP_K * n_tile * V7X_SUBLANES, V7X_LANES), F32),
                        pltpu.SemaphoreType.DMA((2,))])
    return pl.pallas_call(
        functools.partial(_combine_kernel, tile0=tile0, n_exp=n_exp, runs_per_expert=runs_per_expert, final=final),
        grid_spec=grid_spec,
        out_shape=jax.ShapeDtypeStruct((t, d), F32),
        compiler_params=pltpu.CompilerParams(dimension_semantics=("arbitrary",),
                                             vmem_limit_bytes=V7X_VMEM_LIMIT_BYTES),
        name="combine",
    )(gdst, glen, off, x1, ys, rt_tall, g2, final_g.reshape(1, d))


def _routing_plan(tile_counts, tile_rows, bm, n_blocks):
    n_tiles, n_exp = tile_counts.shape
    dot = functools.partial(jnp.dot, precision=lax.Precision.HIGHEST)
    below = lambda n: (jnp.arange(n)[:, None] > jnp.arange(n)[None, :]).astype(F32)
    c = tile_counts.astype(F32)
    counts = jnp.sum(c, axis=0)
    pcounts = jnp.ceil(counts / bm) * bm
    pstarts = dot(below(n_exp), pcounts)
    pends = pstarts + pcounts
    base = dot(below(n_tiles), c)
    off = dot(c, below(n_exp).T)
    gdst = (pstarts[None, :] + base).T
    glen = c.T
    gsrc = (tile_rows.astype(F32)[:, None] + off).T
    blk0 = jnp.arange(n_blocks, dtype=F32)[:, None] * bm
    block_e = jnp.minimum(jnp.sum(pends[None, :] <= blk0, axis=1), n_exp - 1)
    pick = (block_e[:, None] == jnp.arange(n_exp)[None, :]).astype(F32)
    gd_b = dot(pick, gdst)[:, :n_tiles - 1]
    ge_b = dot(pick, gdst + glen)[:, :n_tiles - 1]
    jlo = jnp.sum(ge_b <= blk0, axis=1).astype(I32)
    jhi = jnp.sum(gd_b < blk0 + bm, axis=1).astype(I32)
    n_valid = jnp.clip(dot(pick, pstarts + counts) - blk0[:, 0], 0, bm).astype(I32)
    n_active = (pends[-1:] / bm).astype(I32)
    block_e = block_e.astype(I32)
    flat = lambda a: a.reshape(-1).astype(I32)
    return ((block_e, jlo, jhi, n_valid, n_active, flat(gdst), flat(glen), flat(gsrc)),
            (flat(gdst), flat(glen), flat(off.T)))


def kernel(x_prompt, x_sample, state_pool, state_conv, state_lru_conv, state_lru_h, c_prompt, c_sample, ada_w, ada_b, norm1_g, w_in, pool_w, pool_scale, conv_dw, conv_b, conv_ln_g, conv_ln_b, lru_conv_w, lru_conv_b, lru_wa, lru_ba, lru_wx, lru_bx, lru_lambda, w_out, norm2_g, router_w, router_b, moe_w_gate, moe_b_gate, moe_w_up, moe_b_up, moe_w_down, moe_b_down, final_g):
    p = dict(norm1_g=norm1_g, w_in=w_in, pool_w=pool_w, pool_scale=pool_scale, conv_dw=conv_dw, conv_b=conv_b,
             conv_ln_g=conv_ln_g, conv_ln_b=conv_ln_b, lru_conv_w=lru_conv_w, lru_conv_b=lru_conv_b, lru_wa=lru_wa,
             lru_ba=lru_ba, lru_wx=lru_wx, lru_bx=lru_bx, lru_lambda=lru_lambda, w_out=w_out, norm2_g=norm2_g,
             router_w=router_w, router_b=router_b)
    n_b, seq, d_model = x_prompt.shape
    n_s = x_sample.shape[0]
    depth = ada_w.shape[0]
    n_exp = router_w.shape[2]
    t_p = n_b * seq
    t_all = t_p + n_s
    ts = SEQ_TILE
    n_tiles_p = t_p // ts
    bm = MOE_BLOCK
    n_blocks = -(-(t_all * TOP_K) // bm) + n_exp
    tile_rows = jnp.concatenate([jnp.arange(n_tiles_p, dtype=I32) * (TOP_K * ts), jnp.zeros((1,), I32)])

    mod = _adaln(jnp.concatenate([c_prompt, c_sample], axis=0), ada_w, ada_b)
    xp = x_prompt.reshape(t_p, d_model)
    xs = x_sample.reshape(n_s, d_model)

    new_p, new_s = [], []
    for l in range(depth):
        weights = _layer_weights(p, l)
        mod_p = mod[l, :n_b].reshape(n_b, 6, d_model)
        mod_s = mod[l, n_b:]
        x1p, xl_p, rt_p, cnt_p, npool, nconv, nlc, nlh = _mix_prompt(xp, mod_p, weights, n_b, seq)
        x1s, xl_s, rt_s, cnt_s, spool, sconv, slc, slh = _mix_sample(
            xs, mod_s, weights, state_pool[l], state_conv[l], state_lru_conv[l], state_lru_h[l], PAST_LEN)
        new_p.append((npool, nconv, nlc, nlh.reshape(n_b, -1)))
        new_s.append((spool, sconv, slc, slh))

        tile_counts = jnp.concatenate([cnt_p[:, 0].reshape(n_tiles_p, n_exp), cnt_s[:, 0].reshape(1, n_exp)],
                                      axis=0).astype(I32)
        plan, tables = _routing_plan(tile_counts, tile_rows, bm, n_blocks)
        ys = _experts(l, xl_p, xl_s, plan, moe_w_gate, moe_w_up, moe_w_down, moe_b_gate, moe_b_up, moe_b_down,
                      n_blocks, bm)
        final = l == depth - 1
        g2_p = mod[l, :n_b, 5 * d_model:]
        g2_s = mod[l, n_b:, 5 * d_model:]
        xp = _combine(x1p, ys, rt_p, g2_p, final_g, tables, 0, ts, seq, final, n_exp)
        xs = _combine(x1s, ys, rt_s, g2_s, final_g, tables, n_tiles_p, n_s, 1, final, n_exp)

    stack = lambda items, i: jnp.stack([it[i] for it in items])
    return (xp.reshape(n_b, seq, d_model), xs.reshape(n_s, 1, d_model),
            stack(new_p, 0), stack(new_p, 1), stack(new_p, 2), stack(new_p, 3),
            stack(new_s, 0), stack(new_s, 1), stack(new_s, 2), stack(new_s, 3))
```
